```python
import math
import jax, jax.numpy as jnp
from jax import lax
import numpy as np

D_MODEL = 1024
BATCH = 32
SEQ = 2048
DEPTH = 1
DEC_BATCH = 16
DEC_SEQ = 4096
PAST_LEN = 128

N_ATT_HEADS = 8
QK_NOPE = 64
QK_ROPE = 32
V_HEAD = 64
Q_LORA = 384
KV_LORA = 256
ATT_WIDTH = N_ATT_HEADS * V_HEAD
ROPE_BASE = 10000.0
Q_BLOCK = 128
LRU_WIDTH = 512
LRU_BLOCKS = 8
LRU_BLOCK = LRU_WIDTH // LRU_BLOCKS
CONV_W = 4
LRU_C = 8.0
IN_WIDTH = Q_LORA + KV_LORA + QK_ROPE + 2 * LRU_WIDTH
MIX_WIDTH = ATT_WIDTH + LRU_WIDTH
N_GROUPS = 4
EXPERTS_PER_GROUP = 8
N_EXPERTS = N_GROUPS * EXPERTS_PER_GROUP
TOP_K_IN_GROUP = 2
D_EXPERT = 256
EPS = 1e-6

kernel_name = "hymba_rglru_mla_hier_moe_encoder"


def rmsnorm(x, g):
    xf = x.astype(jnp.float32)
    out = xf * lax.rsqrt(jnp.mean(xf * xf, axis=-1, keepdims=True) + EPS)
    return (out * g.astype(jnp.float32)).astype(x.dtype)


def rope_tables(s):
    pos = jnp.arange(s, dtype=jnp.float32)
    inv_freq = 1.0 / (ROPE_BASE ** (jnp.arange(0, QK_ROPE, 2, dtype=jnp.float32) / QK_ROPE))
    freqs = pos[:, None] * inv_freq[None, :]
    return jnp.cos(freqs), jnp.sin(freqs)


def apply_rope(x, cos, sin):
    x1, x2 = jnp.split(x.astype(jnp.float32), 2, axis=-1)
    out = jnp.concatenate([x1 * cos - x2 * sin, x2 * cos + x1 * sin], axis=-1)
    return out.astype(x.dtype)


def mla(q_lat, kv_lat, k_rope, q_norm, w_q_up, kv_norm, w_kv_up):
    b, s, _ = q_lat.shape
    nb = s // Q_BLOCK
    q = (rmsnorm(q_lat, q_norm) @ w_q_up).reshape(b, s, N_ATT_HEADS, QK_NOPE + QK_ROPE)
    kv = (rmsnorm(kv_lat, kv_norm) @ w_kv_up).reshape(b, s, N_ATT_HEADS, QK_NOPE + V_HEAD)
    q_nope, q_pe = q[..., :QK_NOPE], q[..., QK_NOPE:]
    k_nope, v = kv[..., :QK_NOPE], kv[..., QK_NOPE:]
    cos, sin = rope_tables(s)
    q_pe = apply_rope(q_pe, cos[None, :, None, :], sin[None, :, None, :])
    k_pe = apply_rope(k_rope, cos[None], sin[None])
    scale = (QK_NOPE + QK_ROPE) ** -0.5
    qn_blocks = q_nope.reshape(b, nb, Q_BLOCK, N_ATT_HEADS, QK_NOPE).transpose(1, 0, 2, 3, 4)
    qp_blocks = q_pe.reshape(b, nb, Q_BLOCK, N_ATT_HEADS, QK_ROPE).transpose(1, 0, 2, 3, 4)

    def block(args):
        qn, qp = args
        scores = (jnp.einsum('bqhd,bkhd->bhqk', qn, k_nope).astype(jnp.float32)
                  + jnp.einsum('bqhr,bkr->bhqk', qp, k_pe).astype(jnp.float32)) * scale
        p = jax.nn.softmax(scores, axis=-1).astype(v.dtype)
        return jnp.einsum('bhqk,bkhd->bqhd', p, v)

    out = lax.map(block, (qn_blocks, qp_blocks))
    return out.transpose(1, 0, 2, 3, 4).reshape(b, s, ATT_WIDTH)


def centred_depthwise_conv(x, w, bias):
    pad_l = (CONV_W - 1) // 2
    pad_r = CONV_W - 1 - pad_l
    rhs = w.astype(x.dtype)[:, None, :]
    out = lax.conv_general_dilated(x, rhs, window_strides=(1,), padding=[(pad_l, pad_r)],
                                   dimension_numbers=('NWC', 'WIO', 'NWC'),
                                   feature_group_count=x.shape[-1])
    return out + bias.astype(x.dtype)


def _lin_combine(e1, e2):
    a1, b1 = e1
    a2, b2 = e2
    return a1 * a2, a2 * b1 + b2


def rglru_direction(x, w_r, b_r, w_i, b_i, lam, reverse):
    b, s, _ = x.shape
    xb = x.reshape(b, s, LRU_BLOCKS, LRU_BLOCK)
    r = jax.nn.sigmoid(jnp.einsum('bsnc,ncd->bsnd', xb, w_r.astype(jnp.float32)).reshape(b, s, LRU_WIDTH)
                       + b_r.astype(jnp.float32))
    i = jax.nn.sigmoid(jnp.einsum('bsnc,ncd->bsnd', xb, w_i.astype(jnp.float32)).reshape(b, s, LRU_WIDTH)
                       + b_i.astype(jnp.float32))
    log_a = -LRU_C * r * jax.nn.softplus(-lam.astype(jnp.float32))
    a = jnp.exp(log_a)
    u = jnp.sqrt(-jnp.expm1(2.0 * log_a)) * (i * x)
    _, h = lax.associative_scan(_lin_combine, (a, u), reverse=reverse, axis=1)
    return h


def rglru_mixer(x_br, g_br, conv_w, conv_b, w_rg, b_rg, w_ig, b_ig, lru_lambda):
    xc = centred_depthwise_conv(x_br, conv_w, conv_b).astype(jnp.float32)
    h = (rglru_direction(xc, w_rg[0], b_rg[0], w_ig[0], b_ig[0], lru_lambda[0], False)
         + rglru_direction(xc, w_rg[1], b_rg[1], w_ig[1], b_ig[1], lru_lambda[1], True))
    return (h * jax.nn.gelu(g_br.astype(jnp.float32))).astype(x_br.dtype)


def hier_moe(x, w_group, b_group, w_router, b_router, w_gate, w_up, w_down):
    b, s, d = x.shape
    t = x.reshape(-1, d)
    n_tok = t.shape[0]
    g_logits = (t @ w_group + b_group).astype(jnp.float32)
    g_prob = jax.nn.softmax(g_logits, axis=-1)
    g_sel = jnp.argmax(g_logits, axis=-1)
    g_weight = jnp.take_along_axis(g_prob, g_sel[:, None], axis=-1)
    e_logits = (t @ w_router + b_router).astype(jnp.float32).reshape(n_tok, N_GROUPS, EXPERTS_PER_GROUP)
    e_in = jnp.take_along_axis(e_logits, g_sel[:, None, None], axis=1)[:, 0]
    top_v, top_i = lax.top_k(e_in, TOP_K_IN_GROUP)
    top_w = jax.nn.softmax(top_v, axis=-1) * g_weight
    expert_idx = g_sel[:, None] * EXPERTS_PER_GROUP + top_i
    gates = jnp.sum(jax.nn.one_hot(expert_idx, N_EXPERTS, dtype=jnp.float32) * top_w[..., None], axis=1)
    out = jnp.zeros((n_tok, d), jnp.float32)
    for e in range(N_EXPERTS):
        h = jax.nn.silu(t @ w_gate[e]) * (t @ w_up[e])
        out = out + gates[:, e:e + 1] * (h @ w_down[e]).astype(jnp.float32)
    return out.astype(x.dtype).reshape(b, s, d)


def encoder_layer(x, norm1, w_in, q_norm, w_q_up, kv_norm, w_kv_up, conv_w, conv_b,
                  w_rg, b_rg, w_ig, b_ig, lru_lambda, att_out_norm, lru_out_norm, w_out,
                  norm2, w_group, b_group, w_router, b_router, w_gate, w_up, w_down):
    u = rmsnorm(x, norm1)
    z = u @ w_in
    c1 = Q_LORA
    c2 = c1 + KV_LORA
    c3 = c2 + QK_ROPE
    c4 = c3 + LRU_WIDTH
    q_lat, kv_lat, k_rope, x_br, g_br = jnp.split(z, [c1, c2, c3, c4], axis=-1)
    att = mla(q_lat, kv_lat, k_rope, q_norm, w_q_up, kv_norm, w_kv_up)
    lru = rglru_mixer(x_br, g_br, conv_w, conv_b, w_rg, b_rg, w_ig, b_ig, lru_lambda)
    mixed = jnp.concatenate([rmsnorm(att, att_out_norm), rmsnorm(lru, lru_out_norm)], axis=-1)
    x = x + mixed @ w_out
    x = x + hier_moe(rmsnorm(x, norm2), w_group, b_group, w_router, b_router, w_gate, w_up, w_down)
    return x


def trunk(x, norm1, w_in, q_norm, w_q_up, kv_norm, w_kv_up, conv_w, conv_b,
          w_rg, b_rg, w_ig, b_ig, lru_lambda, att_out_norm, lru_out_norm, w_out,
          norm2, w_group, b_group, w_router, b_router, w_gate, w_up, w_down, final_norm):
    for l in range(DEPTH):
        x = encoder_layer(x, norm1[l], w_in[l], q_norm[l], w_q_up[l], kv_norm[l], w_kv_up[l],
                          conv_w[l], conv_b[l], w_rg[l], b_rg[l], w_ig[l], b_ig[l], lru_lambda[l],
                          att_out_norm[l], lru_out_norm[l], w_out[l], norm2[l], w_group[l], b_group[l],
                          w_router[l], b_router[l], w_gate[l], w_up[l], w_down[l])
    return rmsnorm(x, final_norm)


def setup_inputs(seed: int = 0) -> dict:
    key = jax.random.key(seed)
    ks = jax.random.split(key, 32)
    f32 = jnp.float32

    def nrm(k, shape, scale):
        return jax.random.normal(k, shape, f32) * scale

    def gain(k, shape):
        return 1.0 + 0.01 * jax.random.normal(k, shape, f32)

    ac = jax.random.uniform(ks[14], (DEPTH, 2, LRU_WIDTH), f32, 0.9, 0.999)
    a = ac ** (1.0 / LRU_C)
    lru_lambda = jnp.log(a) - jnp.log1p(-a)
    return {
        "x_prompt": jax.random.normal(ks[0], (BATCH, SEQ, D_MODEL), f32),
        "x_sample": jax.random.normal(ks[1], (DEC_BATCH, DEC_SEQ, D_MODEL), f32),
        "norm1": gain(ks[2], (DEPTH, D_MODEL)),
        "w_in": nrm(ks[3], (DEPTH, D_MODEL, IN_WIDTH), D_MODEL ** -0.5),
        "q_norm": gain(ks[4], (DEPTH, Q_LORA)),
        "w_q_up": nrm(ks[5], (DEPTH, Q_LORA, N_ATT_HEADS * (QK_NOPE + QK_ROPE)), Q_LORA ** -0.5),
        "kv_norm": gain(ks[6], (DEPTH, KV_LORA)),
        "w_kv_up": nrm(ks[7], (DEPTH, KV_LORA, N_ATT_HEADS * (QK_NOPE + V_HEAD)), KV_LORA ** -0.5),
        "conv_w": nrm(ks[8], (DEPTH, CONV_W, LRU_WIDTH), CONV_W ** -0.5),
        "conv_b": nrm(ks[9], (DEPTH, LRU_WIDTH), 0.01),
        "w_rg": nrm(ks[10], (DEPTH, 2, LRU_BLOCKS, LRU_BLOCK, LRU_BLOCK), LRU_BLOCK ** -0.5),
        "b_rg": nrm(ks[11], (DEPTH, 2, LRU_WIDTH), 0.01),
        "w_ig": nrm(ks[12], (DEPTH, 2, LRU_BLOCKS, LRU_BLOCK, LRU_BLOCK), LRU_BLOCK ** -0.5),
        "b_ig": nrm(ks[13], (DEPTH, 2, LRU_WIDTH), 0.01),
        "lru_lambda": lru_lambda,
        "att_out_norm": gain(ks[15], (DEPTH, ATT_WIDTH)),
        "lru_out_norm": gain(ks[16], (DEPTH, LRU_WIDTH)),
        "w_out": nrm(ks[17], (DEPTH, MIX_WIDTH, D_MODEL), MIX_WIDTH ** -0.5),
        "norm2": gain(ks[18], (DEPTH, D_MODEL)),
        "w_group": nrm(ks[19], (DEPTH, D_MODEL, N_GROUPS), D_MODEL ** -0.5),
        "b_group": nrm(ks[20], (DEPTH, N_GROUPS), 0.01),
        "w_router": nrm(ks[21], (DEPTH, D_MODEL, N_EXPERTS), D_MODEL ** -0.5),
        "b_router": nrm(ks[22], (DEPTH, N_EXPERTS), 0.01),
        "w_gate": nrm(ks[23], (DEPTH, N_EXPERTS, D_MODEL, D_EXPERT), D_MODEL ** -0.5),
        "w_up": nrm(ks[24], (DEPTH, N_EXPERTS, D_MODEL, D_EXPERT), D_MODEL ** -0.5),
        "w_down": nrm(ks[25], (DEPTH, N_EXPERTS, D_EXPERT, D_MODEL), D_EXPERT ** -0.5),
        "final_norm": gain(ks[26], (D_MODEL,)),
    }


def reference(x_prompt, x_sample, norm1, w_in, q_norm, w_q_up, kv_norm, w_kv_up, conv_w, conv_b,
              w_rg, b_rg, w_ig, b_ig, lru_lambda, att_out_norm, lru_out_norm, w_out,
              norm2, w_group, b_group, w_router, b_router, w_gate, w_up, w_down, final_norm):
    y_prompt = trunk(x_prompt, norm1, w_in, q_norm, w_q_up, kv_norm, w_kv_up, conv_w, conv_b,
                     w_rg, b_rg, w_ig, b_ig, lru_lambda, att_out_norm, lru_out_norm, w_out,
                     norm2, w_group, b_group, w_router, b_router, w_gate, w_up, w_down, final_norm)
    y_sample = trunk(x_sample, norm1, w_in, q_norm, w_q_up, kv_norm, w_kv_up, conv_w, conv_b,
                     w_rg, b_rg, w_ig, b_ig, lru_lambda, att_out_norm, lru_out_norm, w_out,
                     norm2, w_group, b_group, w_router, b_router, w_gate, w_up, w_down, final_norm)
    return (y_prompt, y_sample)
```

```python
import functools
import math

import jax
import jax.numpy as jnp
from jax import lax
from jax.experimental import pallas as pl
from jax.experimental.pallas import tpu as pltpu

F32 = jnp.float32
BF16 = jnp.bfloat16

D_MODEL = 1024
N_HEADS = 8
N_PAIRS = N_HEADS // 2
QK_NOPE = 64
QK_ROPE = 32
V_HEAD = 64
HEAD_PAD = 128
Q_LORA = 384
KV_LORA = 256
ATT_WIDTH = N_HEADS * V_HEAD
ROPE_BASE = 10000.0
LRU_WIDTH = 512
LRU_BLOCKS = 8
LRU_BLOCK = LRU_WIDTH // LRU_BLOCKS
CONV_W = 4
LRU_C = 8.0
N_GROUPS = 4
EXPERTS_PER_GROUP = 8
N_EXPERTS = N_GROUPS * EXPERTS_PER_GROUP
D_EXPERT = 256
EPS = 1e-6
LANES = 128

C_QLAT = 0
C_KVLAT = C_QLAT + Q_LORA
C_XBR = C_KVLAT + KV_LORA
C_GBR = C_XBR + LRU_WIDTH
C_KR = C_GBR + LRU_WIDTH
C_KRS = C_KR + HEAD_PAD
IN_PAD = C_KRS + HEAD_PAD

VMEM_LIMIT = 56 * 1024 * 1024


def _rms(x):
    return x * lax.rsqrt(jnp.mean(x * x, axis=-1, keepdims=True) + EPS)


def _sigmoid(x):
    return 1.0 / (1.0 + jnp.exp(-x))


def _gelu_tanh(x):
    return 0.5 * x * (1.0 + jnp.tanh(math.sqrt(2.0 / math.pi) * (x + 0.044715 * (x * x * x))))


def _params(sem):
    return pltpu.CompilerParams(dimension_semantics=sem, vmem_limit_bytes=VMEM_LIMIT)


def _proj_in_kernel(x_ref, n1_ref, win_ref, qn_ref, wq_ref, kvn_ref, wk_ref, wv_ref,
                    cq_ref, sq_ref, ck_ref, sk_ref,
                    q_ref, k_ref, v_ref, xbr_ref, gbr_ref):
    x = x_ref[...]
    u = _rms(x) * n1_ref[...]
    z = jnp.dot(u.astype(BF16), win_ref[...], preferred_element_type=F32)
    xbr_ref[...] = z[:, C_XBR:C_XBR + LRU_WIDTH].astype(BF16)
    gbr_ref[...] = z[:, C_GBR:C_GBR + LRU_WIDTH].astype(BF16)

    qn = (_rms(z[:, C_QLAT:C_QLAT + Q_LORA]) * qn_ref[...]).astype(BF16)
    q2 = jnp.dot(qn, wq_ref[...], preferred_element_type=F32)
    kvn = (_rms(z[:, C_KVLAT:C_KVLAT + KV_LORA]) * kvn_ref[...]).astype(BF16)
    kk = jnp.dot(kvn, wk_ref[...], preferred_element_type=F32)
    vv = jnp.dot(kvn, wv_ref[...], preferred_element_type=F32)

    cq, sq = cq_ref[...], sq_ref[...]
    kpe = z[:, C_KR:C_KR + HEAD_PAD] * ck_ref[...] + z[:, C_KRS:C_KRS + HEAD_PAD] * sk_ref[...]
    half = N_HEADS * HEAD_PAD
    for p in range(N_PAIRS):
        qs, ks = [], []
        for j in range(2):
            c0 = (2 * p + j) * HEAD_PAD
            qs.append(q2[:, c0:c0 + HEAD_PAD] * cq + q2[:, half + c0:half + c0 + HEAD_PAD] * sq)
            ks.append(kk[:, c0:c0 + HEAD_PAD] + kpe)
        q_ref[p] = jnp.concatenate(qs, axis=-1).astype(BF16)
        k_ref[p] = jnp.concatenate(ks, axis=-1).astype(BF16)
        v_ref[p] = vv[:, p * LANES:(p + 1) * LANES].astype(BF16)


def _proj_in(x2, seq, w, tm):
    t_tok = x2.shape[0]
    nt = t_tok // tm
    per_seq = seq // tm
    full = lambda a: pl.BlockSpec(a.shape, lambda i: (0,) * a.ndim)
    tab = pl.BlockSpec((tm, HEAD_PAD), lambda i: (i % per_seq, 0))
    pair_out = lambda width: pl.BlockSpec((N_PAIRS, tm, width), lambda i: (0, i, 0))
    row_out = pl.BlockSpec((tm, LRU_WIDTH), lambda i: (i, 0))
    return pl.pallas_call(
        _proj_in_kernel,
        grid=(nt,),
        in_specs=[pl.BlockSpec((tm, D_MODEL), lambda i: (i, 0)),
                  full(w["norm1"]), full(w["w_in"]), full(w["q_norm"]), full(w["w_q"]),
                  full(w["kv_norm"]), full(w["w_k"]), full(w["w_v"]), tab, tab, tab, tab],
        out_specs=[pair_out(2 * HEAD_PAD), pair_out(2 * HEAD_PAD), pair_out(LANES), row_out, row_out],
        out_shape=[jax.ShapeDtypeStruct((N_PAIRS, t_tok, 2 * HEAD_PAD), BF16),
                   jax.ShapeDtypeStruct((N_PAIRS, t_tok, 2 * HEAD_PAD), BF16),
                   jax.ShapeDtypeStruct((N_PAIRS, t_tok, LANES), BF16),
                   jax.ShapeDtypeStruct((t_tok, LRU_WIDTH), BF16),
                   jax.ShapeDtypeStruct((t_tok, LRU_WIDTH), BF16)],
        compiler_params=_params(("arbitrary",)),
        name="proj_in",
    )(x2, w["norm1"], w["w_in"], w["q_norm"], w["w_q"], w["kv_norm"], w["w_k"], w["w_v"],
      w["cq"], w["sq"], w["ck"], w["sk"])


def _attn_kernel(q_ref, k_ref, v_ref, o_ref):
    v = v_ref[0]
    outs = []
    for j in range(2):
        q = q_ref[0, :, j * HEAD_PAD:(j + 1) * HEAD_PAD]
        k = k_ref[0, :, j * HEAD_PAD:(j + 1) * HEAD_PAD]
        s = lax.dot_general(q, k, (((1,), (1,)), ((), ())), preferred_element_type=F32)
        m = jnp.max(s, axis=-1, keepdims=True)
        p = jnp.exp(s - m)
        l = jnp.sum(p, axis=-1, keepdims=True)
        o = jnp.dot(p.astype(BF16), v, preferred_element_type=F32)
        outs.append(o / l)
    lane = lax.broadcasted_iota(jnp.int32, outs[0].shape, 1)
    o_ref[...] = jnp.where(lane < V_HEAD, outs[0], outs[1]).astype(BF16)


def _attention(q, k, v, batch, seq, tq):
    t_tok = q.shape[1]
    nq = seq // tq
    return pl.pallas_call(
        _attn_kernel,
        grid=(batch, N_PAIRS, nq),
        in_specs=[pl.BlockSpec((1, tq, 2 * HEAD_PAD), lambda b, p, i: (p, b * nq + i, 0)),
                  pl.BlockSpec((1, seq, 2 * HEAD_PAD), lambda b, p, i: (p, b, 0)),
                  pl.BlockSpec((1, seq, LANES), lambda b, p, i: (p, b, 0))],
        out_specs=pl.BlockSpec((tq, LANES), lambda b, p, i: (b * nq + i, p)),
        out_shape=jax.ShapeDtypeStruct((t_tok, ATT_WIDTH), BF16),
        compiler_params=_params(("arbitrary", "arbitrary", "arbitrary")),
        name="attn",
    )(q, k, v)


def _lru_kernel(x_ref, g_ref, cw_ref, cb_ref, wg_ref, bg_ref, lam_ref, on_ref, o_ref,
                xpad, hsum, af, uf, ab, ub, *, seq, n):
    nc = seq // n
    assert nc % 2 == 0 and nc * n == seq
    zeros8 = jnp.zeros((8, LRU_WIDTH), F32)
    xpad[0:8, :] = zeros8
    xpad[seq + 8:seq + 16, :] = zeros8

    def fill(ci, carry):
        t0 = pl.multiple_of(ci * n, n)
        xpad[pl.ds(t0 + 8, n), :] = x_ref[pl.ds(t0, n), :].astype(F32)
        return carry

    lax.fori_loop(0, nc, fill, 0)

    lam = lam_ref[...]
    neg = -lam
    softplus = jnp.maximum(neg, 0.0) + jnp.log1p(jnp.exp(-jnp.abs(neg)))
    c_all = -LRU_C * softplus

    def coeffs(t0, d):
        xw = xpad[pl.ds(t0, n + 16), :]
        xc = cb_ref[...] + cw_ref[0:1, :] * xw[7:7 + n]
        xc = xc + cw_ref[1:2, :] * xw[8:8 + n]
        xc = xc + cw_ref[2:3, :] * xw[9:9 + n]
        xc = xc + cw_ref[3:4, :] * xw[10:10 + n]
        w0 = 2 * LRU_WIDTH * d
        pre = jnp.dot(xc.astype(BF16), wg_ref[:, w0:w0 + 2 * LRU_WIDTH],
                      preferred_element_type=F32) + bg_ref[:, w0:w0 + 2 * LRU_WIDTH]
        r = _sigmoid(pre[:, :LRU_WIDTH])
        i = _sigmoid(pre[:, LRU_WIDTH:])
        a = jnp.exp(c_all[d:d + 1, :] * r)
        u = jnp.sqrt(1.0 - a * a) * (i * xc)
        return a, u

    def pair(ci, carry, *, first):
        t0f = pl.multiple_of(ci * n, n)
        t0b = pl.multiple_of((nc - 1 - ci) * n, n)
        a, u = coeffs(t0f, 0)
        af[...] = a
        uf[...] = u
        a, u = coeffs(t0b, 1)
        ab[...] = a
        ub[...] = u

        def step(j, c2):
            h1, h2 = c2
            jb = n - 1 - j
            h1 = af[pl.ds(j, 1), :] * h1 + uf[pl.ds(j, 1), :]
            h2 = ab[pl.ds(jb, 1), :] * h2 + ub[pl.ds(jb, 1), :]
            uf[pl.ds(j, 1), :] = h1
            ub[pl.ds(jb, 1), :] = h2
            return h1, h2

        carry = lax.fori_loop(0, n, step, carry, unroll=8)
        if first:
            hsum[pl.ds(t0f, n), :] = uf[...]
            hsum[pl.ds(t0b, n), :] = ub[...]
        else:
            hsum[pl.ds(t0f, n), :] += uf[...]
            hsum[pl.ds(t0b, n), :] += ub[...]
        return carry

    h0 = jnp.zeros((1, LRU_WIDTH), F32)
    carry = lax.fori_loop(0, nc // 2, functools.partial(pair, first=True), (h0, h0))
    lax.fori_loop(nc // 2, nc, functools.partial(pair, first=False), carry)

    def fin(ci, carry):
        t0 = pl.multiple_of(ci * n, n)
        h = hsum[pl.ds(t0, n), :]
        y = h * _gelu_tanh(g_ref[pl.ds(t0, n), :].astype(F32))
        o_ref[pl.ds(t0, n), :] = (_rms(y) * on_ref[...]).astype(BF16)
        return carry

    lax.fori_loop(0, nc, fin, 0)


def _lru(xbr, gbr, w, batch, seq, n):
    t_tok = xbr.shape[0]
    full = lambda a: pl.BlockSpec(a.shape, lambda b: (0,) * a.ndim)
    blk = pl.BlockSpec((seq, LRU_WIDTH), lambda b: (b, 0))
    return pl.pallas_call(
        functools.partial(_lru_kernel, seq=seq, n=n),
        grid=(batch,),
        in_specs=[blk, blk, full(w["conv_w"]), full(w["conv_b"]), full(w["w_gates"]),
                  full(w["b_gates"]), full(w["lru_lambda"]), full(w["lru_out_norm"])],
        out_specs=blk,
        out_shape=jax.ShapeDtypeStruct((t_tok, LRU_WIDTH), BF16),
        scratch_shapes=[pltpu.VMEM((seq + 16, LRU_WIDTH), F32),
                        pltpu.VMEM((seq, LRU_WIDTH), F32),
                        pltpu.VMEM((n, LRU_WIDTH), F32),
                        pltpu.VMEM((n, LRU_WIDTH), F32),
                        pltpu.VMEM((n, LRU_WIDTH), F32),
                        pltpu.VMEM((n, LRU_WIDTH), F32)],
        compiler_params=_params(("arbitrary",)),
        name="lru",
    )(xbr, gbr, w["conv_w"], w["conv_b"], w["w_gates"], w["b_gates"], w["lru_lambda"],
      w["lru_out_norm"])


def _post_kernel(x_ref, att_ref, lru_ref, an_ref, wo_ref, n2_ref, wr_ref, br_ref,
                 x1_ref, t_ref, gates_ref):
    att = att_ref[...].astype(F32)
    attn = (_rms(att) * an_ref[...]).astype(BF16)
    mixed = jnp.concatenate([attn, lru_ref[...]], axis=-1)
    x1 = x_ref[...] + jnp.dot(mixed, wo_ref[...], preferred_element_type=F32)
    x1_ref[...] = x1
    t = _rms(x1) * n2_ref[...]
    t_hi = t.astype(BF16)
    t_lo = (t - t_hi.astype(F32)).astype(BF16)
    t_ref[...] = t_hi
    lg = (jnp.dot(t_hi, wr_ref[...], preferred_element_type=F32)
          + jnp.dot(t_lo, wr_ref[...], preferred_element_type=F32))
    logits = lg[:, :LANES] + lg[:, LANES:] + br_ref[...]

    lane = lax.broadcasted_iota(jnp.int32, logits.shape, 1)
    lanef = lane.astype(F32)
    big = jnp.float32(1e9)
    ninf = jnp.float32(-jnp.inf)
    gmask = (lane >= N_EXPERTS) & (lane < N_EXPERTS + N_GROUPS)
    gl = jnp.where(gmask, logits, ninf)
    gmax = jnp.max(gl, axis=-1, keepdims=True)
    gsel = jnp.min(jnp.where(gl == gmax, lanef, big), axis=-1, keepdims=True) - N_EXPERTS
    gw = 1.0 / jnp.sum(jnp.where(gmask, jnp.exp(gl - gmax), 0.0), axis=-1, keepdims=True)
    lane_group = (lane // EXPERTS_PER_GROUP).astype(F32)
    emask = (lane < N_EXPERTS) & (lane_group == gsel)
    el = jnp.where(emask, logits, ninf)
    v1 = jnp.max(el, axis=-1, keepdims=True)
    i1 = jnp.min(jnp.where(el == v1, lanef, big), axis=-1, keepdims=True)
    el2 = jnp.where(lanef == i1, ninf, el)
    v2 = jnp.max(el2, axis=-1, keepdims=True)
    i2 = jnp.min(jnp.where(el2 == v2, lanef, big), axis=-1, keepdims=True)
    e2 = jnp.exp(v2 - v1)
    w1 = 1.0 / (1.0 + e2)
    w2 = e2 / (1.0 + e2)
    gates_ref[...] = (jnp.where(lanef == i1, w1 * gw, 0.0) + jnp.where(lanef == i2, w2 * gw, 0.0))


def _post(x2, att, lru, w, tm):
    t_tok = x2.shape[0]
    full = lambda a: pl.BlockSpec(a.shape, lambda i: (0,) * a.ndim)
    row = lambda width: pl.BlockSpec((tm, width), lambda i: (i, 0))
    return pl.pallas_call(
        _post_kernel,
        grid=(t_tok // tm,),
        in_specs=[row(D_MODEL), row(ATT_WIDTH), row(LRU_WIDTH), full(w["att_out_norm"]),
                  full(w["w_out"]), full(w["norm2"]), full(w["w_route"]), full(w["b_route"])],
        out_specs=[row(D_MODEL), row(D_MODEL), row(LANES)],
        out_shape=[jax.ShapeDtypeStruct((t_tok, D_MODEL), F32),
                   jax.ShapeDtypeStruct((t_tok, D_MODEL), BF16),
                   jax.ShapeDtypeStruct((t_tok, LANES), F32)],
        compiler_params=_params(("arbitrary",)),
        name="post",
    )(x2, att, lru, w["att_out_norm"], w["w_out"], w["norm2"], w["w_route"], w["b_route"])


def _moe_kernel(t_ref, g_ref, x1_ref, wgu_ref, wd_ref, fn_ref, o_ref, acc):
    e = pl.program_id(1)

    @pl.when(e == 0)
    def _():
        acc[...] = jnp.zeros_like(acc)

    gu = jnp.dot(t_ref[...], wgu_ref[0], preferred_element_type=F32)
    a = gu[:, :D_EXPERT]
    h = (a * _sigmoid(a)) * gu[:, D_EXPERT:]
    y = jnp.dot(h.astype(BF16), wd_ref[0], preferred_element_type=F32)
    g = g_ref[...]
    lane = lax.broadcasted_iota(jnp.int32, g.shape, 1)
    ge = jnp.sum(jnp.where(lane == e, g, 0.0), axis=-1, keepdims=True)
    acc[...] += ge * y

    @pl.when(e == N_EXPERTS - 1)
    def _():
        o_ref[...] = _rms(x1_ref[...] + acc[...]) * fn_ref[...]


def _moe(t, gates, x1, w, tm):
    t_tok = t.shape[0]
    row = lambda width: pl.BlockSpec((tm, width), lambda i, e: (i, 0))
    return pl.pallas_call(
        _moe_kernel,
        grid=(t_tok // tm, N_EXPERTS),
        in_specs=[row(D_MODEL), row(LANES), row(D_MODEL),
                  pl.BlockSpec((1, D_MODEL, 2 * D_EXPERT), lambda i, e: (e, 0, 0)),
                  pl.BlockSpec((1, D_EXPERT, D_MODEL), lambda i, e: (e, 0, 0)),
                  pl.BlockSpec((1, D_MODEL), lambda i, e: (0, 0))],
        out_specs=row(D_MODEL),
        out_shape=jax.ShapeDtypeStruct((t_tok, D_MODEL), F32),
        scratch_shapes=[pltpu.VMEM((tm, D_MODEL), F32)],
        compiler_params=_params(("arbitrary", "arbitrary")),
        name="moe",
    )(t, gates, x1, w["w_gu"], w["w_down"], w["final_norm"])


def _rope_tables(seq):
    pos = jnp.arange(seq, dtype=F32)
    inv_freq = 1.0 / (ROPE_BASE ** (jnp.arange(0, QK_ROPE, 2, dtype=F32) / QK_ROPE))
    freqs = pos[:, None] * inv_freq[None, :]
    cos2 = jnp.concatenate([jnp.cos(freqs), jnp.cos(freqs)], axis=-1)
    sin2 = jnp.concatenate([jnp.sin(freqs), jnp.sin(freqs)], axis=-1)
    scale = (QK_NOPE + QK_ROPE) ** -0.5
    pad = jnp.zeros((seq, HEAD_PAD - QK_NOPE - QK_ROPE), F32)
    ones = jnp.ones((seq, QK_NOPE), F32)
    zeros = jnp.zeros((seq, QK_NOPE), F32)
    cq = jnp.concatenate([ones, cos2, pad], axis=-1) * scale
    sq = jnp.concatenate([zeros, sin2, pad], axis=-1) * scale
    ck = jnp.concatenate([zeros, cos2, pad], axis=-1)
    sk = jnp.concatenate([zeros, sin2, pad], axis=-1)
    return cq, sq, ck, sk


def _rot_half_cols(wp):
    hr = QK_ROPE // 2
    return jnp.concatenate([-wp[..., hr:], wp[..., :hr]], axis=-1)


def _layout_weights(norm1, w_in, q_norm, w_q_up, kv_norm, w_kv_up, conv_w, conv_b, w_rg, b_rg,
                    w_ig, b_ig, lru_lambda, att_out_norm, lru_out_norm, w_out, norm2, w_group,
                    b_group, w_router, b_router, w_gate, w_up, w_down, final_norm):
    row = lambda a: a.reshape(1, -1)
    c1 = Q_LORA
    c2 = c1 + KV_LORA
    c3 = c2 + QK_ROPE
    c4 = c3 + LRU_WIDTH
    w_kr = w_in[:, c2:c3]
    zl = jnp.zeros((D_MODEL, QK_NOPE), F32)
    zr = jnp.zeros((D_MODEL, HEAD_PAD - QK_NOPE - QK_ROPE), F32)
    w_in2 = jnp.concatenate([w_in[:, :c2], w_in[:, c3:c4], w_in[:, c4:],
                             zl, w_kr, zr, zl, _rot_half_cols(w_kr), zr], axis=-1)

    wq = w_q_up.reshape(Q_LORA, N_HEADS, QK_NOPE + QK_ROPE)
    zq = jnp.zeros((Q_LORA, N_HEADS, HEAD_PAD - QK_NOPE - QK_ROPE), F32)
    wq_plain = jnp.concatenate([wq, zq], axis=-1).reshape(Q_LORA, N_HEADS * HEAD_PAD)
    wq_rot = jnp.concatenate([jnp.zeros((Q_LORA, N_HEADS, QK_NOPE), F32),
                              _rot_half_cols(wq[..., QK_NOPE:]), zq], axis=-1)
    w_q = jnp.concatenate([wq_plain, wq_rot.reshape(Q_LORA, N_HEADS * HEAD_PAD)], axis=-1)

    wkv = w_kv_up.reshape(KV_LORA, N_HEADS, QK_NOPE + V_HEAD)
    w_k = jnp.concatenate([wkv[..., :QK_NOPE],
                           jnp.zeros((KV_LORA, N_HEADS, HEAD_PAD - QK_NOPE), F32)],
                          axis=-1).reshape(KV_LORA, N_HEADS * HEAD_PAD)
    w_v = wkv[..., QK_NOPE:].reshape(KV_LORA, N_HEADS * V_HEAD)

    def block_diag(wb):
        eye = jnp.eye(LRU_BLOCKS, dtype=F32)
        return jnp.einsum('ncd,nm->ncmd', wb, eye).reshape(LRU_WIDTH, LRU_WIDTH)

    w_gates = jnp.concatenate([block_diag(w_rg[0]), block_diag(w_ig[0]),
                               block_diag(w_rg[1]), block_diag(w_ig[1])], axis=-1)
    b_gates = jnp.concatenate([b_rg[0], b_ig[0], b_rg[1], b_ig[1]]).reshape(1, -1)

    w_r = jnp.concatenate([w_router, w_group,
                           jnp.zeros((D_MODEL, LANES - N_EXPERTS - N_GROUPS), F32)], axis=-1)
    w_r_hi = w_r.astype(BF16)
    w_r_lo = (w_r - w_r_hi.astype(F32)).astype(BF16)
    b_r = jnp.concatenate([b_router, b_group,
                           jnp.zeros((LANES - N_EXPERTS - N_GROUPS,), F32)]).reshape(1, -1)

    return {
        "norm1": row(norm1), "w_in": w_in2.astype(BF16), "q_norm": row(q_norm),
        "w_q": w_q.astype(BF16), "kv_norm": row(kv_norm), "w_k": w_k.astype(BF16),
        "w_v": w_v.astype(BF16),
        "conv_w": conv_w, "conv_b": row(conv_b), "w_gates": w_gates.astype(BF16),
        "b_gates": b_gates, "lru_lambda": lru_lambda, "lru_out_norm": row(lru_out_norm),
        "att_out_norm": row(att_out_norm), "w_out": w_out.astype(BF16), "norm2": row(norm2),
        "w_route": jnp.concatenate([w_r_hi, w_r_lo], axis=-1), "b_route": b_r,
        "w_gu": jnp.concatenate([w_gate, w_up], axis=-1).astype(BF16),
        "w_down": w_down.astype(BF16), "final_norm": row(final_norm),
    }


def _tile(n, pref):
    return pref if n % pref == 0 else n


def _trunk(x, w):
    batch, seq, _ = x.shape
    t_tok = batch * seq
    x2 = x.reshape(t_tok, D_MODEL)
    wt = dict(w)
    wt["cq"], wt["sq"], wt["ck"], wt["sk"] = _rope_tables(seq)
    q, k, v, xbr, gbr = _proj_in(x2, seq, wt, _tile(seq, 512))
    att = _attention(q, k, v, batch, seq, _tile(seq, 256))
    lru = _lru(xbr, gbr, wt, batch, seq, min(512, seq // 2))
    x1, t, gates = _post(x2, att, lru, wt, _tile(t_tok, 512))
    y = _moe(t, gates, x1, wt, _tile(t_tok, 1024))
    return y.reshape(batch, seq, D_MODEL)


def kernel(x_prompt, x_sample, norm1, w_in, q_norm, w_q_up, kv_norm, w_kv_up, conv_w, conv_b, w_rg, b_rg, w_ig, b_ig, lru_lambda, att_out_norm, lru_out_norm, w_out, norm2, w_group, b_group, w_router, b_router, w_gate, w_up, w_down, final_norm):
    w = _layout_weights(norm1[0], w_in[0], q_norm[0], w_q_up[0], kv_norm[0], w_kv_up[0], conv_w[0],
                        conv_b[0], w_rg[0], b_rg[0], w_ig[0], b_ig[0], lru_lambda[0],
                        att_out_norm[0], lru_out_norm[0], w_out[0], norm2[0], w_group[0],
                        b_group[0], w_router[0], b_router[0], w_gate[0], w_up[0], w_down[0],
                        final_norm)
    return (_trunk(x_prompt, w), _trunk(x_sample, w))
```

```python
import functools
import math

import jax
import jax.numpy as jnp
from jax import lax
from jax.experimental import pallas as pl
from jax.experimental.pallas import tpu as pltpu

F32 = jnp.float32
BF16 = jnp.bfloat16

D_MODEL = 1024
N_HEADS = 8
N_PAIRS = N_HEADS // 2
QK_NOPE = 64
QK_ROPE = 32
V_HEAD = 64
HEAD_PAD = 128
Q_LORA = 384
KV_LORA = 256
ATT_WIDTH = N_HEADS * V_HEAD
ROPE_BASE = 10000.0
LRU_WIDTH = 512
LRU_BLOCKS = 8
LRU_BLOCK = LRU_WIDTH // LRU_BLOCKS
CONV_W = 4
LRU_C = 8.0
N_GROUPS = 4
EXPERTS_PER_GROUP = 8
N_EXPERTS = N_GROUPS * EXPERTS_PER_GROUP
D_EXPERT = 256
EPS = 1e-6
LANES = 128

C_QLAT = 0
C_KVLAT = C_QLAT + Q_LORA
C_XBR = C_KVLAT + KV_LORA
C_GBR = C_XBR + LRU_WIDTH
C_KR = C_GBR + LRU_WIDTH
C_KRS = C_KR + HEAD_PAD
IN_PAD = C_KRS + HEAD_PAD

VMEM_LIMIT = 56 * 1024 * 1024


def _rms(x):
    return x * lax.rsqrt(jnp.mean(x * x, axis=-1, keepdims=True) + EPS)


def _sigmoid(x):
    return 1.0 / (1.0 + jnp.exp(-x))


def _gelu_tanh(x):
    return 0.5 * x * (1.0 + jnp.tanh(math.sqrt(2.0 / math.pi) * (x + 0.044715 * (x * x * x))))


def _params(sem):
    return pltpu.CompilerParams(dimension_semantics=sem, vmem_limit_bytes=VMEM_LIMIT)


SLAB = D_MODEL // LANES


def _store_slabs(ref, row0, x):
    n = x.shape[0]
    for s in range(SLAB):
        ref[pl.ds(row0 + s, n, stride=SLAB), :] = x[:, s * LANES:(s + 1) * LANES]


def _load_slabs(ref, row0, n):
    return jnp.concatenate([ref[pl.ds(row0 + s, n, stride=SLAB), :] for s in range(SLAB)], axis=-1)


def _proj_in_kernel(x_ref, n1_ref, win_ref, qn_ref, wq_ref, kvn_ref, wk_ref, wv_ref,
                    cq_ref, sq_ref, ck_ref, sk_ref,
                    q_ref, k_ref, v_ref, xbr_ref, gbr_ref):
    x = x_ref[...]
    u = _rms(x) * n1_ref[...]
    z = jnp.dot(u.astype(BF16), win_ref[...], preferred_element_type=F32)
    xbr_ref[...] = z[:, C_XBR:C_XBR + LRU_WIDTH].astype(BF16)
    gbr_ref[...] = z[:, C_GBR:C_GBR + LRU_WIDTH].astype(BF16)

    qn = (_rms(z[:, C_QLAT:C_QLAT + Q_LORA]) * qn_ref[...]).astype(BF16)
    q2 = jnp.dot(qn, wq_ref[...], preferred_element_type=F32)
    kvn = (_rms(z[:, C_KVLAT:C_KVLAT + KV_LORA]) * kvn_ref[...]).astype(BF16)
    kk = jnp.dot(kvn, wk_ref[...], preferred_element_type=F32)
    vv = jnp.dot(kvn, wv_ref[...], preferred_element_type=F32)

    cq, sq = cq_ref[...], sq_ref[...]
    kpe = z[:, C_KR:C_KR + HEAD_PAD] * ck_ref[...] + z[:, C_KRS:C_KRS + HEAD_PAD] * sk_ref[...]
    half = N_HEADS * HEAD_PAD
    for p in range(N_PAIRS):
        qs, ks = [], []
        for j in range(2):
            c0 = (2 * p + j) * HEAD_PAD
            qs.append(q2[:, c0:c0 + HEAD_PAD] * cq + q2[:, half + c0:half + c0 + HEAD_PAD] * sq)
            ks.append(kk[:, c0:c0 + HEAD_PAD] + kpe)
        q_ref[p] = jnp.concatenate(qs, axis=-1).astype(BF16)
        k_ref[p] = jnp.concatenate(ks, axis=-1).astype(BF16)
        v_ref[p] = vv[:, p * LANES:(p + 1) * LANES].astype(BF16)


def _proj_in(x2, seq, w, tm):
    t_tok = x2.shape[0]
    nt = t_tok // tm
    per_seq = seq // tm
    full = lambda a: pl.BlockSpec(a.shape, lambda i: (0,) * a.ndim)
    tab = pl.BlockSpec((tm, HEAD_PAD), lambda i: (i % per_seq, 0))
    pair_out = lambda width: pl.BlockSpec((N_PAIRS, tm, width), lambda i: (0, i, 0))
    row_out = pl.BlockSpec((tm, LRU_WIDTH), lambda i: (i, 0))
    return pl.pallas_call(
        _proj_in_kernel,
        grid=(nt,),
        in_specs=[pl.BlockSpec((tm, D_MODEL), lambda i: (i, 0)),
                  full(w["norm1"]), full(w["w_in"]), full(w["q_norm"]), full(w["w_q"]),
                  full(w["kv_norm"]), full(w["w_k"]), full(w["w_v"]), tab, tab, tab, tab],
        out_specs=[pair_out(2 * HEAD_PAD), pair_out(2 * HEAD_PAD), pair_out(LANES), row_out, row_out],
        out_shape=[jax.ShapeDtypeStruct((N_PAIRS, t_tok, 2 * HEAD_PAD), BF16),
                   jax.ShapeDtypeStruct((N_PAIRS, t_tok, 2 * HEAD_PAD), BF16),
                   jax.ShapeDtypeStruct((N_PAIRS, t_tok, LANES), BF16),
                   jax.ShapeDtypeStruct((t_tok, LRU_WIDTH), BF16),
                   jax.ShapeDtypeStruct((t_tok, LRU_WIDTH), BF16)],
        compiler_params=_params(("arbitrary",)),
        name="proj_in",
    )(x2, w["norm1"], w["w_in"], w["q_norm"], w["w_q"], w["kv_norm"], w["w_k"], w["w_v"],
      w["cq"], w["sq"], w["ck"], w["sk"])


def _attn_kernel(q_ref, k_ref, v_ref, o_ref):
    v = v_ref[0]
    outs = []
    for j in range(2):
        q = q_ref[0, :, j * HEAD_PAD:(j + 1) * HEAD_PAD]
        k = k_ref[0, :, j * HEAD_PAD:(j + 1) * HEAD_PAD]
        s = lax.dot_general(q, k, (((1,), (1,)), ((), ())), preferred_element_type=F32)
        m = jnp.max(s, axis=-1, keepdims=True)
        p = jnp.exp(s - m)
        l = jnp.sum(p, axis=-1, keepdims=True)
        o = jnp.dot(p.astype(BF16), v, preferred_element_type=F32)
        outs.append(o / l)
    lane = lax.broadcasted_iota(jnp.int32, outs[0].shape, 1)
    o_ref[...] = jnp.where(lane < V_HEAD, outs[0], outs[1]).astype(BF16)


def _attention(q, k, v, batch, seq, tq):
    t_tok = q.shape[1]
    nq = seq // tq
    return pl.pallas_call(
        _attn_kernel,
        grid=(batch, N_PAIRS, nq),
        in_specs=[pl.BlockSpec((1, tq, 2 * HEAD_PAD), lambda b, p, i: (p, b * nq + i, 0)),
                  pl.BlockSpec((1, seq, 2 * HEAD_PAD), lambda b, p, i: (p, b, 0)),
                  pl.BlockSpec((1, seq, LANES), lambda b, p, i: (p, b, 0))],
        out_specs=pl.BlockSpec((tq, LANES), lambda b, p, i: (b * nq + i, p)),
        out_shape=jax.ShapeDtypeStruct((t_tok, ATT_WIDTH), BF16),
        compiler_params=_params(("arbitrary", "arbitrary", "arbitrary")),
        name="attn",
    )(q, k, v)


def _lru_kernel(x_ref, g_ref, cw_ref, cb_ref, wg_ref, bg_ref, lam_ref, on_ref, o_ref,
                xpad, hsum, af, uf, ab, ub, *, seq, n):
    nc = seq // n
    assert nc % 2 == 0 and nc * n == seq
    zeros8 = jnp.zeros((8, LRU_WIDTH), F32)
    xpad[0:8, :] = zeros8
    xpad[seq + 8:seq + 16, :] = zeros8

    def fill(ci, carry):
        t0 = pl.multiple_of(ci * n, n)
        xpad[pl.ds(t0 + 8, n), :] = x_ref[pl.ds(t0, n), :].astype(F32)
        return carry

    lax.fori_loop(0, nc, fill, 0)

    lam = lam_ref[...]
    neg = -lam
    softplus = jnp.maximum(neg, 0.0) + jnp.log1p(jnp.exp(-jnp.abs(neg)))
    c_all = -LRU_C * softplus

    def coeffs(t0, d):
        xw = xpad[pl.ds(t0, n + 16), :]
        xc = cb_ref[...] + cw_ref[0:1, :] * xw[7:7 + n]
        xc = xc + cw_ref[1:2, :] * xw[8:8 + n]
        xc = xc + cw_ref[2:3, :] * xw[9:9 + n]
        xc = xc + cw_ref[3:4, :] * xw[10:10 + n]
        w0 = 2 * LRU_WIDTH * d
        pre = jnp.dot(xc.astype(BF16), wg_ref[:, w0:w0 + 2 * LRU_WIDTH],
                      preferred_element_type=F32) + bg_ref[:, w0:w0 + 2 * LRU_WIDTH]
        r = _sigmoid(pre[:, :LRU_WIDTH])
        i = _sigmoid(pre[:, LRU_WIDTH:])
        a = jnp.exp(c_all[d:d + 1, :] * r)
        u = jnp.sqrt(1.0 - a * a) * (i * xc)
        return a, u

    def pair(ci, carry, *, first):
        t0f = pl.multiple_of(ci * n, n)
        t0b = pl.multiple_of((nc - 1 - ci) * n, n)
        a, u = coeffs(t0f, 0)
        af[...] = a
        uf[...] = u
        a, u = coeffs(t0b, 1)
        ab[...] = a
        ub[...] = u

        def step(j, c2):
            h1, h2 = c2
            jb = n - 1 - j
            h1 = af[pl.ds(j, 1), :] * h1 + uf[pl.ds(j, 1), :]
            h2 = ab[pl.ds(jb, 1), :] * h2 + ub[pl.ds(jb, 1), :]
            uf[pl.ds(j, 1), :] = h1
            ub[pl.ds(jb, 1), :] = h2
            return h1, h2

        carry = lax.fori_loop(0, n, step, carry, unroll=8)
        if first:
            hsum[pl.ds(t0f, n), :] = uf[...]
            hsum[pl.ds(t0b, n), :] = ub[...]
        else:
            hsum[pl.ds(t0f, n), :] += uf[...]
            hsum[pl.ds(t0b, n), :] += ub[...]
        return carry

    h0 = jnp.zeros((1, LRU_WIDTH), F32)
    carry = lax.fori_loop(0, nc // 2, functools.partial(pair, first=True), (h0, h0))
    lax.fori_loop(nc // 2, nc, functools.partial(pair, first=False), carry)

    def fin(ci, carry):
        t0 = pl.multiple_of(ci * n, n)
        h = hsum[pl.ds(t0, n), :]
        y = h * _gelu_tanh(g_ref[pl.ds(t0, n), :].astype(F32))
        o_ref[pl.ds(t0, n), :] = (_rms(y) * on_ref[...]).astype(BF16)
        return carry

    lax.fori_loop(0, nc, fin, 0)


def _lru(xbr, gbr, w, batch, seq, n):
    t_tok = xbr.shape[0]
    full = lambda a: pl.BlockSpec(a.shape, lambda b: (0,) * a.ndim)
    blk = pl.BlockSpec((seq, LRU_WIDTH), lambda b: (b, 0))
    return pl.pallas_call(
        functools.partial(_lru_kernel, seq=seq, n=n),
        grid=(batch,),
        in_specs=[blk, blk, full(w["conv_w"]), full(w["conv_b"]), full(w["w_gates"]),
                  full(w["b_gates"]), full(w["lru_lambda"]), full(w["lru_out_norm"])],
        out_specs=blk,
        out_shape=jax.ShapeDtypeStruct((t_tok, LRU_WIDTH), BF16),
        scratch_shapes=[pltpu.VMEM((seq + 16, LRU_WIDTH), F32),
                        pltpu.VMEM((seq, LRU_WIDTH), F32),
                        pltpu.VMEM((n, LRU_WIDTH), F32),
                        pltpu.VMEM((n, LRU_WIDTH), F32),
                        pltpu.VMEM((n, LRU_WIDTH), F32),
                        pltpu.VMEM((n, LRU_WIDTH), F32)],
        compiler_params=_params(("arbitrary",)),
        name="lru",
    )(xbr, gbr, w["conv_w"], w["conv_b"], w["w_gates"], w["b_gates"], w["lru_lambda"],
      w["lru_out_norm"])


def _post_kernel(x_ref, att_ref, lru_ref, an_ref, wo_ref, n2_ref, wr_ref, br_ref,
                 x1_ref, t_ref, route_ref):
    att = att_ref[...].astype(F32)
    attn = (_rms(att) * an_ref[...]).astype(BF16)
    mixed = jnp.concatenate([attn, lru_ref[...]], axis=-1)
    x1 = x_ref[...] + jnp.dot(mixed, wo_ref[...], preferred_element_type=F32)
    x1_ref[...] = x1
    t = _rms(x1) * n2_ref[...]
    _store_slabs(t_ref, 0, t)
    t_hi = t.astype(BF16)
    t_lo = (t - t_hi.astype(F32)).astype(BF16)
    lg = (jnp.dot(t_hi, wr_ref[...], preferred_element_type=F32)
          + jnp.dot(t_lo, wr_ref[...], preferred_element_type=F32))
    logits = lg[:, :LANES] + lg[:, LANES:] + br_ref[...]

    lane = lax.broadcasted_iota(jnp.int32, logits.shape, 1)
    lanef = lane.astype(F32)
    big = jnp.float32(1e9)
    ninf = jnp.float32(-jnp.inf)
    gmask = (lane >= N_EXPERTS) & (lane < N_EXPERTS + N_GROUPS)
    gl = jnp.where(gmask, logits, ninf)
    gmax = jnp.max(gl, axis=-1, keepdims=True)
    gsel = jnp.min(jnp.where(gl == gmax, lanef, big), axis=-1, keepdims=True) - N_EXPERTS
    gw = 1.0 / jnp.sum(jnp.where(gmask, jnp.exp(gl - gmax), 0.0), axis=-1, keepdims=True)
    lane_group = (lane // EXPERTS_PER_GROUP).astype(F32)
    emask = (lane < N_EXPERTS) & (lane_group == gsel)
    el = jnp.where(emask, logits, ninf)
    v1 = jnp.max(el, axis=-1, keepdims=True)
    i1 = jnp.min(jnp.where(el == v1, lanef, big), axis=-1, keepdims=True)
    el2 = jnp.where(lanef == i1, ninf, el)
    v2 = jnp.max(el2, axis=-1, keepdims=True)
    i2 = jnp.min(jnp.where(el2 == v2, lanef, big), axis=-1, keepdims=True)
    e2 = jnp.exp(v2 - v1)
    w1 = (1.0 / (1.0 + e2)) * gw
    w2 = (e2 / (1.0 + e2)) * gw
    route_ref[...] = (jnp.where(lane == 0, i1, 0.0) + jnp.where(lane == 1, i2, 0.0)
                      + jnp.where(lane == 2, w1, 0.0) + jnp.where(lane == 3, w2, 0.0))


def _post(x2, att, lru, w, tm):
    t_tok = x2.shape[0]
    full = lambda a: pl.BlockSpec(a.shape, lambda i: (0,) * a.ndim)
    row = lambda width: pl.BlockSpec((tm, width), lambda i: (i, 0))
    return pl.pallas_call(
        _post_kernel,
        grid=(t_tok // tm,),
        in_specs=[row(D_MODEL), row(ATT_WIDTH), row(LRU_WIDTH), full(w["att_out_norm"]),
                  full(w["w_out"]), full(w["norm2"]), full(w["w_route"]), full(w["b_route"])],
        out_specs=[row(D_MODEL), pl.BlockSpec((tm * SLAB, LANES), lambda i: (i, 0)), row(LANES)],
        out_shape=[jax.ShapeDtypeStruct((t_tok, D_MODEL), F32),
                   jax.ShapeDtypeStruct((t_tok * SLAB, LANES), F32),
                   jax.ShapeDtypeStruct((t_tok, LANES), F32)],
        compiler_params=_params(("arbitrary",)),
        name="post",
    )(x2, att, lru, w["att_out_norm"], w["w_out"], w["norm2"], w["w_route"], w["b_route"])


def _route_plan(route, tm, tr):
    t_tok = route.shape[0]
    ids = route[:, 0:2].astype(jnp.int32)
    flat = ids.T.reshape(-1)
    onehot = (flat[:, None] == jnp.arange(N_EXPERTS, dtype=jnp.int32)[None, :]).astype(jnp.int32)
    incl = jnp.cumsum(onehot, axis=0)
    counts = incl[-1]
    rank = jnp.sum((incl - onehot) * onehot, axis=-1)
    padded = ((counts + tr - 1) // tr) * tr
    ends = jnp.cumsum(padded)
    starts = ends - padded
    pos = (starts[flat] + rank).reshape(2, t_tok // tm, tm)
    pos_tiles = pos.transpose(1, 0, 2).reshape(-1)
    zero_start = jnp.where(padded > 0, ends - tr, -1).astype(jnp.int32)
    n_tiles = (2 * t_tok) // tr + N_EXPERTS
    tile_first = jnp.arange(n_tiles, dtype=jnp.int32) * tr
    tile_expert = jnp.minimum(jnp.searchsorted(ends, tile_first, side="right"),
                              N_EXPERTS - 1).astype(jnp.int32)
    n_used = (ends[-1] // tr).astype(jnp.int32).reshape(1)
    return pos_tiles.astype(jnp.int32), zero_start, tile_expert, n_used


def _row_copy(src, src_row, dst, dst_row, sem):
    s0 = pl.multiple_of(src_row * SLAB, SLAB)
    d0 = pl.multiple_of(dst_row * SLAB, SLAB)
    return pltpu.make_async_copy(src.at[pl.ds(s0, SLAB)], dst.at[pl.ds(d0, SLAB)], sem)


def _dispatch_kernel(zs_ref, pos_ref, t_hbm, xs_hbm, zbuf, sem, zsem, *, tm, tr):
    i = pl.program_id(0)
    n = pl.num_programs(0)

    def zero_copy(e):
        z0 = pl.multiple_of(zs_ref[e] * SLAB, tr * SLAB)
        return pltpu.make_async_copy(zbuf, xs_hbm.at[pl.ds(z0, tr * SLAB)], zsem)

    @pl.when(i == 0)
    def _():
        zbuf[...] = jnp.zeros_like(zbuf)
        for e in range(N_EXPERTS):
            @pl.when(zs_ref[e] >= 0)
            def _():
                zero_copy(e).start()
        for e in range(N_EXPERTS):
            @pl.when(zs_ref[e] >= 0)
            def _():
                zero_copy(e).wait()

    def issue(j, carry):
        tok = i * tm + j
        _row_copy(t_hbm, tok, xs_hbm, pos_ref[j], sem).start()
        _row_copy(t_hbm, tok, xs_hbm, pos_ref[tm + j], sem).start()
        return carry

    lax.fori_loop(0, tm, issue, 0, unroll=8)

    def drain(j, carry):
        _row_copy(t_hbm, 0, xs_hbm, 0, sem).wait()
        _row_copy(t_hbm, 0, xs_hbm, 0, sem).wait()
        return carry

    @pl.when(i > 0)
    def _():
        lax.fori_loop(0, tm, drain, 0, unroll=8)

    @pl.when(i == n - 1)
    def _():
        lax.fori_loop(0, tm, drain, 0, unroll=8)


def _dispatch(t, pos_tiles, zero_start, n_rows, tm, tr):
    t_tok = t.shape[0] // SLAB
    return pl.pallas_call(
        functools.partial(_dispatch_kernel, tm=tm, tr=tr),
        grid_spec=pltpu.PrefetchScalarGridSpec(
            num_scalar_prefetch=1,
            grid=(t_tok // tm,),
            in_specs=[pl.BlockSpec((2 * tm,), lambda i, zs: (i,), memory_space=pltpu.SMEM),
                      pl.BlockSpec(memory_space=pl.ANY)],
            out_specs=pl.BlockSpec(memory_space=pl.ANY),
            scratch_shapes=[pltpu.VMEM((tr * SLAB, LANES), F32), pltpu.SemaphoreType.DMA(()),
                            pltpu.SemaphoreType.DMA(())]),
        out_shape=jax.ShapeDtypeStruct((n_rows * SLAB, LANES), F32),
        compiler_params=_params(("arbitrary",)),
        name="dispatch",
    )(zero_start, pos_tiles, t)


def _experts_kernel(te_ref, nu_ref, xs_ref, wgu_ref, wd_ref, ys_ref, *, tr):
    @pl.when(pl.program_id(0) < nu_ref[0])
    def _():
        x = _load_slabs(xs_ref, 0, tr).astype(BF16)
        gu = jnp.dot(x, wgu_ref[0], preferred_element_type=F32)
        a = gu[:, :D_EXPERT]
        h = (a * _sigmoid(a)) * gu[:, D_EXPERT:]
        _store_slabs(ys_ref, 0, jnp.dot(h.astype(BF16), wd_ref[0], preferred_element_type=F32))


def _experts(xs, tile_expert, n_used, w, tr):
    n_rows = xs.shape[0] // SLAB
    row = pl.BlockSpec((tr * SLAB, LANES), lambda i, te, nu: (jnp.minimum(i, nu[0] - 1), 0))
    return pl.pallas_call(
        functools.partial(_experts_kernel, tr=tr),
        grid_spec=pltpu.PrefetchScalarGridSpec(
            num_scalar_prefetch=2,
            grid=(n_rows // tr,),
            in_specs=[row,
                      pl.BlockSpec((1, D_MODEL, 2 * D_EXPERT), lambda i, te, nu: (te[i], 0, 0)),
                      pl.BlockSpec((1, D_EXPERT, D_MODEL), lambda i, te, nu: (te[i], 0, 0))],
            out_specs=row),
        out_shape=jax.ShapeDtypeStruct((n_rows * SLAB, LANES), F32),
        compiler_params=_params(("arbitrary",)),
        name="experts",
    )(tile_expert, n_used, xs, w["w_gu"], w["w_down"])


def _combine_kernel(pos_ref, nxt_ref, x1_ref, route_ref, fn_ref, ys_hbm, o_ref, ybuf, sem, *, tm):
    i = pl.program_id(0)
    n = pl.num_programs(0)
    slot = i % 2

    def issue(p_ref, s):
        def body(j, carry):
            _row_copy(ys_hbm, p_ref[j], ybuf.at[s], j, sem.at[s]).start()
            _row_copy(ys_hbm, p_ref[tm + j], ybuf.at[s], tm + j, sem.at[s]).start()
            return carry
        lax.fori_loop(0, tm, body, 0, unroll=8)

    @pl.when(i == 0)
    def _():
        issue(pos_ref, 0)

    @pl.when(i + 1 < n)
    def _():
        issue(nxt_ref, 1 - slot)

    def drain(j, carry):
        _row_copy(ys_hbm, 0, ybuf.at[slot], 0, sem.at[slot]).wait()
        _row_copy(ys_hbm, 0, ybuf.at[slot], 0, sem.at[slot]).wait()
        return carry

    lax.fori_loop(0, tm, drain, 0, unroll=8)

    r = route_ref[...]
    yb = ybuf.at[slot]
    moe = r[:, 2:3] * _load_slabs(yb, 0, tm) + r[:, 3:4] * _load_slabs(yb, tm * SLAB, tm)
    o_ref[...] = _rms(x1_ref[...] + moe) * fn_ref[...]


def _combine(ys, pos_tiles, x1, route, w, tm):
    t_tok = x1.shape[0]
    nt = t_tok // tm
    row = lambda width: pl.BlockSpec((tm, width), lambda i: (i, 0))
    return pl.pallas_call(
        functools.partial(_combine_kernel, tm=tm),
        grid=(nt,),
        in_specs=[pl.BlockSpec((2 * tm,), lambda i: (i,), memory_space=pltpu.SMEM),
                  pl.BlockSpec((2 * tm,), lambda i: (jnp.minimum(i + 1, nt - 1),),
                               memory_space=pltpu.SMEM),
                  row(D_MODEL), row(LANES), pl.BlockSpec((1, D_MODEL), lambda i: (0, 0)),
                  pl.BlockSpec(memory_space=pl.ANY)],
        out_specs=row(D_MODEL),
        out_shape=jax.ShapeDtypeStruct((t_tok, D_MODEL), F32),
        scratch_shapes=[pltpu.VMEM((2, 2 * tm * SLAB, LANES), F32),
                        pltpu.SemaphoreType.DMA((2,))],
        compiler_params=_params(("arbitrary",)),
        name="combine",
    )(pos_tiles, pos_tiles, x1, route, w["final_norm"], ys)


def _moe(t, route, x1, w, tm, tr):
    t_tok = x1.shape[0]
    n_rows = 2 * t_tok + N_EXPERTS * tr
    pos_tiles, zero_start, tile_expert, n_used = _route_plan(route, tm, tr)
    xs = _dispatch(t, pos_tiles, zero_start, n_rows, tm, tr)
    ys = _experts(xs, tile_expert, n_used, w, tr)
    return _combine(ys, pos_tiles, x1, route, w, tm)


def _rope_tables(seq):
    pos = jnp.arange(seq, dtype=F32)
    inv_freq = 1.0 / (ROPE_BASE ** (jnp.arange(0, QK_ROPE, 2, dtype=F32) / QK_ROPE))
    freqs = pos[:, None] * inv_freq[None, :]
    cos2 = jnp.concatenate([jnp.cos(freqs), jnp.cos(freqs)], axis=-1)
    sin2 = jnp.concatenate([jnp.sin(freqs), jnp.sin(freqs)], axis=-1)
    scale = (QK_NOPE + QK_ROPE) ** -0.5
    pad = jnp.zeros((seq, HEAD_PAD - QK_NOPE - QK_ROPE), F32)
    ones = jnp.ones((seq, QK_NOPE), F32)
    zeros = jnp.zeros((seq, QK_NOPE), F32)
    cq = jnp.concatenate([ones, cos2, pad], axis=-1) * scale
    sq = jnp.concatenate([zeros, sin2, pad], axis=-1) * scale
    ck = jnp.concatenate([zeros, cos2, pad], axis=-1)
    sk = jnp.concatenate([zeros, sin2, pad], axis=-1)
    return cq, sq, ck, sk


def _rot_half_cols(wp):
    hr = QK_ROPE // 2
    return jnp.concatenate([-wp[..., hr:], wp[..., :hr]], axis=-1)


def _layout_weights(norm1, w_in, q_norm, w_q_up, kv_norm, w_kv_up, conv_w, conv_b, w_rg, b_rg,
                    w_ig, b_ig, lru_lambda, att_out_norm, lru_out_norm, w_out, norm2, w_group,
                    b_group, w_router, b_router, w_gate, w_up, w_down, final_norm):
    row = lambda a: a.reshape(1, -1)
    c1 = Q_LORA
    c2 = c1 + KV_LORA
    c3 = c2 + QK_ROPE
    c4 = c3 + LRU_WIDTH
    w_kr = w_in[:, c2:c3]
    zl = jnp.zeros((D_MODEL, QK_NOPE), F32)
    zr = jnp.zeros((D_MODEL, HEAD_PAD - QK_NOPE - QK_ROPE), F32)
    w_in2 = jnp.concatenate([w_in[:, :c2], w_in[:, c3:c4], w_in[:, c4:],
                             zl, w_kr, zr, zl, _rot_half_cols(w_kr), zr], axis=-1)

    wq = w_q_up.reshape(Q_LORA, N_HEADS, QK_NOPE + QK_ROPE)
    zq = jnp.zeros((Q_LORA, N_HEADS, HEAD_PAD - QK_NOPE - QK_ROPE), F32)
    wq_plain = jnp.concatenate([wq, zq], axis=-1).reshape(Q_LORA, N_HEADS * HEAD_PAD)
    wq_rot = jnp.concatenate([jnp.zeros((Q_LORA, N_HEADS, QK_NOPE), F32),
                              _rot_half_cols(wq[..., QK_NOPE:]), zq], axis=-1)
    w_q = jnp.concatenate([wq_plain, wq_rot.reshape(Q_LORA, N_HEADS * HEAD_PAD)], axis=-1)

    wkv = w_kv_up.reshape(KV_LORA, N_HEADS, QK_NOPE + V_HEAD)
    w_k = jnp.concatenate([wkv[..., :QK_NOPE],
                           jnp.zeros((KV_LORA, N_HEADS, HEAD_PAD - QK_NOPE), F32)],
                          axis=-1).reshape(KV_LORA, N_HEADS * HEAD_PAD)
    w_v = wkv[..., QK_NOPE:].reshape(KV_LORA, N_HEADS * V_HEAD)

    def block_diag(wb):
        eye = jnp.eye(LRU_BLOCKS, dtype=F32)
        return jnp.einsum('ncd,nm->ncmd', wb, eye).reshape(LRU_WIDTH, LRU_WIDTH)

    w_gates = jnp.concatenate([block_diag(w_rg[0]), block_diag(w_ig[0]),
                               block_diag(w_rg[1]), block_diag(w_ig[1])], axis=-1)
    b_gates = jnp.concatenate([b_rg[0], b_ig[0], b_rg[1], b_ig[1]]).reshape(1, -1)

    w_r = jnp.concatenate([w_router, w_group,
                           jnp.zeros((D_MODEL, LANES - N_EXPERTS - N_GROUPS), F32)], axis=-1)
    w_r_hi = w_r.astype(BF16)
    w_r_lo = (w_r - w_r_hi.astype(F32)).astype(BF16)
    b_r = jnp.concatenate([b_router, b_group,
                           jnp.zeros((LANES - N_EXPERTS - N_GROUPS,), F32)]).reshape(1, -1)

    return {
        "norm1": row(norm1), "w_in": w_in2.astype(BF16), "q_norm": row(q_norm),
        "w_q": w_q.astype(BF16), "kv_norm": row(kv_norm), "w_k": w_k.astype(BF16),
        "w_v": w_v.astype(BF16),
        "conv_w": conv_w, "conv_b": row(conv_b), "w_gates": w_gates.astype(BF16),
        "b_gates": b_gates, "lru_lambda": lru_lambda, "lru_out_norm": row(lru_out_norm),
        "att_out_norm": row(att_out_norm), "w_out": w_out.astype(BF16), "norm2": row(norm2),
        "w_route": jnp.concatenate([w_r_hi, w_r_lo], axis=-1), "b_route": b_r,
        "w_gu": jnp.concatenate([w_gate, w_up], axis=-1).astype(BF16),
        "w_down": w_down.astype(BF16), "final_norm": row(final_norm),
    }


def _tile(n, pref):
    return pref if n % pref == 0 else n


def _trunk(x, w):
    batch, seq, _ = x.shape
    t_tok = batch * seq
    x2 = x.reshape(t_tok, D_MODEL)
    wt = dict(w)
    wt["cq"], wt["sq"], wt["ck"], wt["sk"] = _rope_tables(seq)
    q, k, v, xbr, gbr = _proj_in(x2, seq, wt, _tile(seq, 512))
    att = _attention(q, k, v, batch, seq, _tile(seq, 256))
    lru = _lru(xbr, gbr, wt, batch, seq, min(512, seq // 2))
    x1, t, route = _post(x2, att, lru, wt, _tile(t_tok, 512))
    y = _moe(t, route, x1, wt, _tile(t_tok, 256), _tile(t_tok, 512))
    return y.reshape(batch, seq, D_MODEL)


def kernel(x_prompt, x_sample, norm1, w_in, q_norm, w_q_up, kv_norm, w_kv_up, conv_w, conv_b, w_rg, b_rg, w_ig, b_ig, lru_lambda, att_out_norm, lru_out_norm, w_out, norm2, w_group, b_group, w_router, b_router, w_gate, w_up, w_down, final_norm):
    w = _layout_weights(norm1[0], w_in[0], q_norm[0], w_q_up[0], kv_norm[0], w_kv_up[0], conv_w[0],
                        conv_b[0], w_rg[0], b_rg[0], w_ig[0], b_ig[0], lru_lambda[0],
                        att_out_norm[0], lru_out_norm[0], w_out[0], norm2[0], w_group[0],
                        b_group[0], w_router[0], b_router[0], w_gate[0], w_up[0], w_down[0],
                        final_norm)
    return (_trunk(x_prompt, w), _trunk(x_sample, w))
```

```python
import functools
import math

import jax
import jax.numpy as jnp
from jax import lax
from jax.experimental import pallas as pl
from jax.experimental.pallas import tpu as pltpu

F32 = jnp.float32
BF16 = jnp.bfloat16

D_MODEL = 1024
N_HEADS = 8
N_PAIRS = N_HEADS // 2
QK_NOPE = 64
QK_ROPE = 32
V_HEAD = 64
HEAD_PAD = 128
Q_LORA = 384
KV_LORA = 256
ATT_WIDTH = N_HEADS * V_HEAD
ROPE_BASE = 10000.0
LRU_WIDTH = 512
LRU_BLOCKS = 8
LRU_BLOCK = LRU_WIDTH // LRU_BLOCKS
CONV_W = 4
LRU_C = 8.0
N_GROUPS = 4
EXPERTS_PER_GROUP = 8
N_EXPERTS = N_GROUPS * EXPERTS_PER_GROUP
D_EXPERT = 256
EPS = 1e-6
LANES = 128

C_QLAT = 0
C_KVLAT = C_QLAT + Q_LORA
C_XBR = C_KVLAT + KV_LORA
C_GBR = C_XBR + LRU_WIDTH
C_KR = C_GBR + LRU_WIDTH
C_KRS = C_KR + HEAD_PAD
IN_PAD = C_KRS + HEAD_PAD

VMEM_LIMIT = 56 * 1024 * 1024


def _rms(x):
    return x * lax.rsqrt(jnp.mean(x * x, axis=-1, keepdims=True) + EPS)


def _sigmoid(x):
    return 1.0 / (1.0 + jnp.exp(-x))


def _gelu_tanh(x):
    return 0.5 * x * (1.0 + jnp.tanh(math.sqrt(2.0 / math.pi) * (x + 0.044715 * (x * x * x))))


def _params(sem):
    return pltpu.CompilerParams(dimension_semantics=sem, vmem_limit_bytes=VMEM_LIMIT)


SLAB = D_MODEL // LANES


def _store_slabs(ref, row0, x):
    n = x.shape[0]
    for s in range(SLAB):
        ref[pl.ds(row0 + s, n, stride=SLAB), :] = x[:, s * LANES:(s + 1) * LANES]


def _load_slabs(ref, row0, n):
    return jnp.concatenate([ref[pl.ds(row0 + s, n, stride=SLAB), :] for s in range(SLAB)], axis=-1)


def _proj_in_kernel(x_ref, n1_ref, win_ref, qn_ref, wq_ref, kvn_ref, wk_ref, wv_ref,
                    cq_ref, sq_ref, ck_ref, sk_ref,
                    q_ref, k_ref, v_ref, xbr_ref, gbr_ref):
    x = x_ref[...]
    u = _rms(x) * n1_ref[...]
    z = jnp.dot(u.astype(BF16), win_ref[...], preferred_element_type=F32)
    xbr_ref[...] = z[:, C_XBR:C_XBR + LRU_WIDTH].astype(BF16)
    gbr_ref[...] = z[:, C_GBR:C_GBR + LRU_WIDTH].astype(BF16)

    qn = (_rms(z[:, C_QLAT:C_QLAT + Q_LORA]) * qn_ref[...]).astype(BF16)
    q2 = jnp.dot(qn, wq_ref[...], preferred_element_type=F32)
    kvn = (_rms(z[:, C_KVLAT:C_KVLAT + KV_LORA]) * kvn_ref[...]).astype(BF16)
    kk = jnp.dot(kvn, wk_ref[...], preferred_element_type=F32)
    vv = jnp.dot(kvn, wv_ref[...], preferred_element_type=F32)

    cq, sq = cq_ref[...], sq_ref[...]
    kpe = z[:, C_KR:C_KR + HEAD_PAD] * ck_ref[...] + z[:, C_KRS:C_KRS + HEAD_PAD] * sk_ref[...]
    half = N_HEADS * HEAD_PAD
    for p in range(N_PAIRS):
        qs, ks = [], []
        for j in range(2):
            c0 = (2 * p + j) * HEAD_PAD
            qs.append(q2[:, c0:c0 + HEAD_PAD] * cq + q2[:, half + c0:half + c0 + HEAD_PAD] * sq)
            ks.append(kk[:, c0:c0 + HEAD_PAD] + kpe)
        q_ref[p] = jnp.concatenate(qs, axis=-1).astype(BF16)
        k_ref[p] = jnp.concatenate(ks, axis=-1).astype(BF16)
        v_ref[p] = vv[:, p * LANES:(p + 1) * LANES].astype(BF16)


def _proj_in(x2, seq, w, tm):
    t_tok = x2.shape[0]
    nt = t_tok // tm
    per_seq = seq // tm
    full = lambda a: pl.BlockSpec(a.shape, lambda i: (0,) * a.ndim)
    tab = pl.BlockSpec((tm, HEAD_PAD), lambda i: (i % per_seq, 0))
    pair_out = lambda width: pl.BlockSpec((N_PAIRS, tm, width), lambda i: (0, i, 0))
    row_out = pl.BlockSpec((tm, LRU_WIDTH), lambda i: (i, 0))
    return pl.pallas_call(
        _proj_in_kernel,
        grid=(nt,),
        in_specs=[pl.BlockSpec((tm, D_MODEL), lambda i: (i, 0)),
                  full(w["norm1"]), full(w["w_in"]), full(w["q_norm"]), full(w["w_q"]),
                  full(w["kv_norm"]), full(w["w_k"]), full(w["w_v"]), tab, tab, tab, tab],
        out_specs=[pair_out(2 * HEAD_PAD), pair_out(2 * HEAD_PAD), pair_out(LANES), row_out, row_out],
        out_shape=[jax.ShapeDtypeStruct((N_PAIRS, t_tok, 2 * HEAD_PAD), BF16),
                   jax.ShapeDtypeStruct((N_PAIRS, t_tok, 2 * HEAD_PAD), BF16),
                   jax.ShapeDtypeStruct((N_PAIRS, t_tok, LANES), BF16),
                   jax.ShapeDtypeStruct((t_tok, LRU_WIDTH), BF16),
                   jax.ShapeDtypeStruct((t_tok, LRU_WIDTH), BF16)],
        compiler_params=_params(("arbitrary",)),
        name="proj_in",
    )(x2, w["norm1"], w["w_in"], w["q_norm"], w["w_q"], w["kv_norm"], w["w_k"], w["w_v"],
      w["cq"], w["sq"], w["ck"], w["sk"])


def _attn_kernel(q_ref, k_ref, v_ref, o_ref, *, pairs):
    for pp in range(pairs):
        v = v_ref[pp]
        outs = []
        for j in range(2):
            q = q_ref[pp, :, j * HEAD_PAD:(j + 1) * HEAD_PAD]
            k = k_ref[pp, :, j * HEAD_PAD:(j + 1) * HEAD_PAD]
            s = lax.dot_general(q, k, (((1,), (1,)), ((), ())), preferred_element_type=F32)
            m = jnp.max(s, axis=-1, keepdims=True)
            p = jnp.exp2(s - m)
            l = jnp.sum(p, axis=-1, keepdims=True)
            o = jnp.dot(p.astype(BF16), v, preferred_element_type=F32)
            outs.append(o / l)
        lane = lax.broadcasted_iota(jnp.int32, outs[0].shape, 1)
        o_ref[:, pp * LANES:(pp + 1) * LANES] = jnp.where(lane < V_HEAD, outs[0],
                                                          outs[1]).astype(BF16)


def _attention(q, k, v, batch, seq, tq, pairs):
    t_tok = q.shape[1]
    nq = seq // tq
    return pl.pallas_call(
        functools.partial(_attn_kernel, pairs=pairs),
        grid=(batch, N_PAIRS // pairs, nq),
        in_specs=[pl.BlockSpec((pairs, tq, 2 * HEAD_PAD), lambda b, p, i: (p, b * nq + i, 0)),
                  pl.BlockSpec((pairs, seq, 2 * HEAD_PAD), lambda b, p, i: (p, b, 0)),
                  pl.BlockSpec((pairs, seq, LANES), lambda b, p, i: (p, b, 0))],
        out_specs=pl.BlockSpec((tq, pairs * LANES), lambda b, p, i: (b * nq + i, p)),
        out_shape=jax.ShapeDtypeStruct((t_tok, ATT_WIDTH), BF16),
        compiler_params=_params(("arbitrary", "arbitrary", "arbitrary")),
        name="attn",
    )(q, k, v)


def _lru_kernel(x_ref, g_ref, cw_ref, cb_ref, wg_ref, bg_ref, lam_ref, on_ref, o_ref,
                xpad, hsum, af, uf, ab, ub, *, seq, n):
    nc = seq // n
    assert nc % 2 == 0 and nc * n == seq
    zeros8 = jnp.zeros((8, LRU_WIDTH), F32)
    xpad[0:8, :] = zeros8
    xpad[seq + 8:seq + 16, :] = zeros8

    def fill(ci, carry):
        t0 = pl.multiple_of(ci * n, n)
        xpad[pl.ds(t0 + 8, n), :] = x_ref[pl.ds(t0, n), :].astype(F32)
        return carry

    lax.fori_loop(0, nc, fill, 0)

    lam = lam_ref[...]
    neg = -lam
    softplus = jnp.maximum(neg, 0.0) + jnp.log1p(jnp.exp(-jnp.abs(neg)))
    c_all = -LRU_C * softplus

    def coeffs(t0, d):
        xw = xpad[pl.ds(t0, n + 16), :]
        xc = cb_ref[...] + cw_ref[0:1, :] * xw[7:7 + n]
        xc = xc + cw_ref[1:2, :] * xw[8:8 + n]
        xc = xc + cw_ref[2:3, :] * xw[9:9 + n]
        xc = xc + cw_ref[3:4, :] * xw[10:10 + n]
        w0 = 2 * LRU_WIDTH * d
        pre = jnp.dot(xc.astype(BF16), wg_ref[:, w0:w0 + 2 * LRU_WIDTH],
                      preferred_element_type=F32) + bg_ref[:, w0:w0 + 2 * LRU_WIDTH]
        r = _sigmoid(pre[:, :LRU_WIDTH])
        i = _sigmoid(pre[:, LRU_WIDTH:])
        a = jnp.exp(c_all[d:d + 1, :] * r)
        u = jnp.sqrt(1.0 - a * a) * (i * xc)
        return a, u

    def pair(ci, carry, *, first):
        t0f = pl.multiple_of(ci * n, n)
        t0b = pl.multiple_of((nc - 1 - ci) * n, n)
        a, u = coeffs(t0f, 0)
        af[...] = a
        uf[...] = u
        a, u = coeffs(t0b, 1)
        ab[...] = a
        ub[...] = u

        def step(j, c2):
            h1, h2 = c2
            jb = n - 1 - j
            h1 = af[pl.ds(j, 1), :] * h1 + uf[pl.ds(j, 1), :]
            h2 = ab[pl.ds(jb, 1), :] * h2 + ub[pl.ds(jb, 1), :]
            uf[pl.ds(j, 1), :] = h1
            ub[pl.ds(jb, 1), :] = h2
            return h1, h2

        carry = lax.fori_loop(0, n, step, carry, unroll=8)
        if first:
            hsum[pl.ds(t0f, n), :] = uf[...]
            hsum[pl.ds(t0b, n), :] = ub[...]
        else:
            hsum[pl.ds(t0f, n), :] += uf[...]
            hsum[pl.ds(t0b, n), :] += ub[...]
        return carry

    h0 = jnp.zeros((1, LRU_WIDTH), F32)
    carry = lax.fori_loop(0, nc // 2, functools.partial(pair, first=True), (h0, h0))
    lax.fori_loop(nc // 2, nc, functools.partial(pair, first=False), carry)

    def fin(ci, carry):
        t0 = pl.multiple_of(ci * n, n)
        h = hsum[pl.ds(t0, n), :]
        y = h * _gelu_tanh(g_ref[pl.ds(t0, n), :].astype(F32))
        o_ref[pl.ds(t0, n), :] = (_rms(y) * on_ref[...]).astype(BF16)
        return carry

    lax.fori_loop(0, nc, fin, 0)


def _lru(xbr, gbr, w, batch, seq, n):
    t_tok = xbr.shape[0]
    full = lambda a: pl.BlockSpec(a.shape, lambda b: (0,) * a.ndim)
    blk = pl.BlockSpec((seq, LRU_WIDTH), lambda b: (b, 0))
    return pl.pallas_call(
        functools.partial(_lru_kernel, seq=seq, n=n),
        grid=(batch,),
        in_specs=[blk, blk, full(w["conv_w"]), full(w["conv_b"]), full(w["w_gates"]),
                  full(w["b_gates"]), full(w["lru_lambda"]), full(w["lru_out_norm"])],
        out_specs=blk,
        out_shape=jax.ShapeDtypeStruct((t_tok, LRU_WIDTH), BF16),
        scratch_shapes=[pltpu.VMEM((seq + 16, LRU_WIDTH), F32),
                        pltpu.VMEM((seq, LRU_WIDTH), F32),
                        pltpu.VMEM((n, LRU_WIDTH), F32),
                        pltpu.VMEM((n, LRU_WIDTH), F32),
                        pltpu.VMEM((n, LRU_WIDTH), F32),
                        pltpu.VMEM((n, LRU_WIDTH), F32)],
        compiler_params=_params(("arbitrary",)),
        name="lru",
    )(xbr, gbr, w["conv_w"], w["conv_b"], w["w_gates"], w["b_gates"], w["lru_lambda"],
      w["lru_out_norm"])


def _post_kernel(x_ref, att_ref, lru_ref, an_ref, wo_ref, n2_ref, wr_ref, br_ref,
                 x1_ref, t_ref, route_ref):
    att = att_ref[...].astype(F32)
    attn = (_rms(att) * an_ref[...]).astype(BF16)
    mixed = jnp.concatenate([attn, lru_ref[...]], axis=-1)
    x1 = x_ref[...] + jnp.dot(mixed, wo_ref[...], preferred_element_type=F32)
    x1_ref[...] = x1
    t = _rms(x1) * n2_ref[...]
    _store_slabs(t_ref, 0, t)
    t_hi = t.astype(BF16)
    t_lo = (t - t_hi.astype(F32)).astype(BF16)
    lg = (jnp.dot(t_hi, wr_ref[...], preferred_element_type=F32)
          + jnp.dot(t_lo, wr_ref[...], preferred_element_type=F32))
    logits = lg[:, :LANES] + lg[:, LANES:] + br_ref[...]

    lane = lax.broadcasted_iota(jnp.int32, logits.shape, 1)
    lanef = lane.astype(F32)
    big = jnp.float32(1e9)
    ninf = jnp.float32(-jnp.inf)
    gmask = (lane >= N_EXPERTS) & (lane < N_EXPERTS + N_GROUPS)
    gl = jnp.where(gmask, logits, ninf)
    gmax = jnp.max(gl, axis=-1, keepdims=True)
    gsel = jnp.min(jnp.where(gl == gmax, lanef, big), axis=-1, keepdims=True) - N_EXPERTS
    gw = 1.0 / jnp.sum(jnp.where(gmask, jnp.exp(gl - gmax), 0.0), axis=-1, keepdims=True)
    lane_group = (lane // EXPERTS_PER_GROUP).astype(F32)
    emask = (lane < N_EXPERTS) & (lane_group == gsel)
    el = jnp.where(emask, logits, ninf)
    v1 = jnp.max(el, axis=-1, keepdims=True)
    i1 = jnp.min(jnp.where(el == v1, lanef, big), axis=-1, keepdims=True)
    el2 = jnp.where(lanef == i1, ninf, el)
    v2 = jnp.max(el2, axis=-1, keepdims=True)
    i2 = jnp.min(jnp.where(el2 == v2, lanef, big), axis=-1, keepdims=True)
    e2 = jnp.exp(v2 - v1)
    w1 = (1.0 / (1.0 + e2)) * gw
    w2 = (e2 / (1.0 + e2)) * gw
    route_ref[...] = (jnp.where(lane == 0, i1, 0.0) + jnp.where(lane == 1, i2, 0.0)
                      + jnp.where(lane == 2, w1, 0.0) + jnp.where(lane == 3, w2, 0.0))


def _post(x2, att, lru, w, tm):
    t_tok = x2.shape[0]
    full = lambda a: pl.BlockSpec(a.shape, lambda i: (0,) * a.ndim)
    row = lambda width: pl.BlockSpec((tm, width), lambda i: (i, 0))
    return pl.pallas_call(
        _post_kernel,
        grid=(t_tok // tm,),
        in_specs=[row(D_MODEL), row(ATT_WIDTH), row(LRU_WIDTH), full(w["att_out_norm"]),
                  full(w["w_out"]), full(w["norm2"]), full(w["w_route"]), full(w["b_route"])],
        out_specs=[row(D_MODEL), pl.BlockSpec((tm * SLAB, LANES), lambda i: (i, 0)), row(LANES)],
        out_shape=[jax.ShapeDtypeStruct((t_tok, D_MODEL), F32),
                   jax.ShapeDtypeStruct((t_tok * SLAB, LANES), F32),
                   jax.ShapeDtypeStruct((t_tok, LANES), F32)],
        compiler_params=_params(("arbitrary",)),
        name="post",
    )(x2, att, lru, w["att_out_norm"], w["w_out"], w["norm2"], w["w_route"], w["b_route"])


def _route_plan(route, tr):
    t_tok = route.shape[0]
    ids = route[:, 0:2].astype(jnp.int32)
    flat = ids.T.reshape(-1)
    onehot = (flat[:, None] == jnp.arange(N_EXPERTS, dtype=jnp.int32)[None, :]).astype(jnp.int32)
    incl = jnp.cumsum(onehot, axis=0)
    counts = incl[-1]
    rank = jnp.sum((incl - onehot) * onehot, axis=-1)
    padded = ((counts + tr - 1) // tr) * tr
    ends = jnp.cumsum(padded)
    starts = ends - padded
    pos = (starts[flat] + rank).astype(jnp.int32).reshape(2, t_tok)
    zero_start = jnp.where(padded > 0, ends - tr, -1).astype(jnp.int32)
    n_tiles = (2 * t_tok) // tr + N_EXPERTS
    tile_first = jnp.arange(n_tiles, dtype=jnp.int32) * tr
    tile_expert = jnp.minimum(jnp.sum((ends[None, :] <= tile_first[:, None]).astype(jnp.int32), axis=-1),
                              N_EXPERTS - 1).astype(jnp.int32)
    n_used = (ends[-1] // tr).astype(jnp.int32).reshape(1)
    return pos, zero_start, tile_expert, n_used


def _tile_major(pos, tm):
    return pos.reshape(2, -1, tm).transpose(1, 0, 2).reshape(-1)


def _row_copy(src, src_row, dst, dst_row, sem):
    s0 = pl.multiple_of(src_row * SLAB, SLAB)
    d0 = pl.multiple_of(dst_row * SLAB, SLAB)
    return pltpu.make_async_copy(src.at[pl.ds(s0, SLAB)], dst.at[pl.ds(d0, SLAB)], sem)


def _dispatch_kernel(zs_ref, pos_ref, t_ref, xs_hbm, zbuf, sem, zsem, *, tm, tr):
    i = pl.program_id(0)

    def zero_copy(e):
        z0 = pl.multiple_of(zs_ref[e] * SLAB, tr * SLAB)
        return pltpu.make_async_copy(zbuf, xs_hbm.at[pl.ds(z0, tr * SLAB)], zsem)

    @pl.when(i == 0)
    def _():
        zbuf[...] = jnp.zeros_like(zbuf)
        for e in range(N_EXPERTS):
            @pl.when(zs_ref[e] >= 0)
            def _():
                zero_copy(e).start()
        for e in range(N_EXPERTS):
            @pl.when(zs_ref[e] >= 0)
            def _():
                zero_copy(e).wait()

    def issue(j, carry):
        _row_copy(t_ref, j, xs_hbm, pos_ref[j], sem).start()
        _row_copy(t_ref, j, xs_hbm, pos_ref[tm + j], sem).start()
        return carry

    lax.fori_loop(0, tm, issue, 0, unroll=8)

    def drain(j, carry):
        _row_copy(t_ref, 0, xs_hbm, 0, sem).wait()
        _row_copy(t_ref, 0, xs_hbm, 0, sem).wait()
        return carry

    lax.fori_loop(0, tm, drain, 0, unroll=8)


def _dispatch(t, pos_tiles, zero_start, n_rows, tm, tr):
    t_tok = t.shape[0] // SLAB
    return pl.pallas_call(
        functools.partial(_dispatch_kernel, tm=tm, tr=tr),
        grid_spec=pltpu.PrefetchScalarGridSpec(
            num_scalar_prefetch=1,
            grid=(t_tok // tm,),
            in_specs=[pl.BlockSpec((2 * tm,), lambda i, zs: (i,), memory_space=pltpu.SMEM),
                      pl.BlockSpec((tm * SLAB, LANES), lambda i, zs: (i, 0))],
            out_specs=pl.BlockSpec(memory_space=pl.ANY),
            scratch_shapes=[pltpu.VMEM((tr * SLAB, LANES), F32), pltpu.SemaphoreType.DMA(()),
                            pltpu.SemaphoreType.DMA(())]),
        out_shape=jax.ShapeDtypeStruct((n_rows * SLAB, LANES), F32),
        compiler_params=_params(("arbitrary",)),
        name="dispatch",
    )(zero_start, pos_tiles, t)


def _experts_kernel(te_ref, nu_ref, xs_ref, wgu_ref, wd_ref, ys_ref, *, tr):
    @pl.when(pl.program_id(0) < nu_ref[0])
    def _():
        x = _load_slabs(xs_ref, 0, tr).astype(BF16)
        gu = jnp.dot(x, wgu_ref[0], preferred_element_type=F32)
        a = gu[:, :D_EXPERT]
        h = (a * _sigmoid(a)) * gu[:, D_EXPERT:]
        _store_slabs(ys_ref, 0, jnp.dot(h.astype(BF16), wd_ref[0], preferred_element_type=F32))


def _experts(xs, tile_expert, n_used, w, tr):
    n_rows = xs.shape[0] // SLAB
    row = pl.BlockSpec((tr * SLAB, LANES), lambda i, te, nu: (jnp.minimum(i, nu[0] - 1), 0))
    return pl.pallas_call(
        functools.partial(_experts_kernel, tr=tr),
        grid_spec=pltpu.PrefetchScalarGridSpec(
            num_scalar_prefetch=2,
            grid=(n_rows // tr,),
            in_specs=[row,
                      pl.BlockSpec((1, D_MODEL, 2 * D_EXPERT), lambda i, te, nu: (te[i], 0, 0)),
                      pl.BlockSpec((1, D_EXPERT, D_MODEL), lambda i, te, nu: (te[i], 0, 0))],
            out_specs=row),
        out_shape=jax.ShapeDtypeStruct((n_rows * SLAB, LANES), F32),
        compiler_params=_params(("arbitrary",)),
        name="experts",
    )(tile_expert, n_used, xs, w["w_gu"], w["w_down"])


def _combine_kernel(pos_ref, nxt_ref, x1_ref, route_ref, fn_ref, ys_hbm, o_ref, ybuf, sem, *, tm):
    i = pl.program_id(0)
    n = pl.num_programs(0)
    slot = i % 2

    def issue(p_ref, s):
        def body(j, carry):
            _row_copy(ys_hbm, p_ref[j], ybuf.at[s], j, sem.at[s]).start()
            _row_copy(ys_hbm, p_ref[tm + j], ybuf.at[s], tm + j, sem.at[s]).start()
            return carry
        lax.fori_loop(0, tm, body, 0, unroll=8)

    @pl.when(i == 0)
    def _():
        issue(pos_ref, 0)

    @pl.when(i + 1 < n)
    def _():
        issue(nxt_ref, 1 - slot)

    def drain(j, carry):
        _row_copy(ys_hbm, 0, ybuf.at[slot], 0, sem.at[slot]).wait()
        _row_copy(ys_hbm, 0, ybuf.at[slot], 0, sem.at[slot]).wait()
        return carry

    lax.fori_loop(0, tm, drain, 0, unroll=8)

    r = route_ref[...]
    yb = ybuf.at[slot]
    moe = r[:, 2:3] * _load_slabs(yb, 0, tm) + r[:, 3:4] * _load_slabs(yb, tm * SLAB, tm)
    o_ref[...] = _rms(x1_ref[...] + moe) * fn_ref[...]


def _combine(ys, pos_tiles, x1, route, w, tm):
    t_tok = x1.shape[0]
    nt = t_tok // tm
    row = lambda width: pl.BlockSpec((tm, width), lambda i: (i, 0))
    return pl.pallas_call(
        functools.partial(_combine_kernel, tm=tm),
        grid=(nt,),
        in_specs=[pl.BlockSpec((2 * tm,), lambda i: (i,), memory_space=pltpu.SMEM),
                  pl.BlockSpec((2 * tm,), lambda i: (jnp.minimum(i + 1, nt - 1),),
                               memory_space=pltpu.SMEM),
                  row(D_MODEL), row(LANES), pl.BlockSpec((1, D_MODEL), lambda i: (0, 0)),
                  pl.BlockSpec(memory_space=pl.ANY)],
        out_specs=row(D_MODEL),
        out_shape=jax.ShapeDtypeStruct((t_tok, D_MODEL), F32),
        scratch_shapes=[pltpu.VMEM((2, 2 * tm * SLAB, LANES), F32),
                        pltpu.SemaphoreType.DMA((2,))],
        compiler_params=_params(("arbitrary",)),
        name="combine",
    )(pos_tiles, pos_tiles, x1, route, w["final_norm"], ys)


def _moe(t, route, x1, w, tm_dispatch, tm_combine, tr):
    t_tok = x1.shape[0]
    n_rows = 2 * t_tok + N_EXPERTS * tr
    pos, zero_start, tile_expert, n_used = _route_plan(route, tr)
    xs = _dispatch(t, _tile_major(pos, tm_dispatch), zero_start, n_rows, tm_dispatch, tr)
    ys = _experts(xs, tile_expert, n_used, w, tr)
    return _combine(ys, _tile_major(pos, tm_combine), x1, route, w, tm_combine)


def _rope_tables(seq):
    pos = jnp.arange(seq, dtype=F32)
    inv_freq = 1.0 / (ROPE_BASE ** (jnp.arange(0, QK_ROPE, 2, dtype=F32) / QK_ROPE))
    freqs = pos[:, None] * inv_freq[None, :]
    cos2 = jnp.concatenate([jnp.cos(freqs), jnp.cos(freqs)], axis=-1)
    sin2 = jnp.concatenate([jnp.sin(freqs), jnp.sin(freqs)], axis=-1)
    scale = (QK_NOPE + QK_ROPE) ** -0.5 * math.log2(math.e)
    pad =jnp.zeros((seq, HEAD_PAD - QK_NOPE - QK_ROPE), F32)
    ones = jnp.ones((seq, QK_NOPE), F32)
    zeros = jnp.zeros((seq, QK_NOPE), F32)
    cq = jnp.concatenate([ones, cos2, pad], axis=-1) * scale
    sq = jnp.concatenate([zeros, sin2, pad], axis=-1) * scale
    ck = jnp.concatenate([zeros, cos2, pad], axis=-1)
    sk = jnp.concatenate([zeros, sin2, pad], axis=-1)
    return cq, sq, ck, sk


def _rot_half_cols(wp):
    hr = QK_ROPE // 2
    return jnp.concatenate([-wp[..., hr:], wp[..., :hr]], axis=-1)


def _layout_weights(norm1, w_in, q_norm, w_q_up, kv_norm, w_kv_up, conv_w, conv_b, w_rg, b_rg,
                    w_ig, b_ig, lru_lambda, att_out_norm, lru_out_norm, w_out, norm2, w_group,
                    b_group, w_router, b_router, w_gate, w_up, w_down, final_norm):
    row = lambda a: a.reshape(1, -1)
    c1 = Q_LORA
    c2 = c1 + KV_LORA
    c3 = c2 + QK_ROPE
    c4 = c3 + LRU_WIDTH
    w_kr = w_in[:, c2:c3]
    zl = jnp.zeros((D_MODEL, QK_NOPE), F32)
    zr = jnp.zeros((D_MODEL, HEAD_PAD - QK_NOPE - QK_ROPE), F32)
    w_in2 = jnp.concatenate([w_in[:, :c2], w_in[:, c3:c4], w_in[:, c4:],
                             zl, w_kr, zr, zl, _rot_half_cols(w_kr), zr], axis=-1)

    wq = w_q_up.reshape(Q_LORA, N_HEADS, QK_NOPE + QK_ROPE)
    zq = jnp.zeros((Q_LORA, N_HEADS, HEAD_PAD - QK_NOPE - QK_ROPE), F32)
    wq_plain = jnp.concatenate([wq, zq], axis=-1).reshape(Q_LORA, N_HEADS * HEAD_PAD)
    wq_rot = jnp.concatenate([jnp.zeros((Q_LORA, N_HEADS, QK_NOPE), F32),
                              _rot_half_cols(wq[..., QK_NOPE:]), zq], axis=-1)
    w_q = jnp.concatenate([wq_plain, wq_rot.reshape(Q_LORA, N_HEADS * HEAD_PAD)], axis=-1)

    wkv = w_kv_up.reshape(KV_LORA, N_HEADS, QK_NOPE + V_HEAD)
    w_k = jnp.concatenate([wkv[..., :QK_NOPE],
                           jnp.zeros((KV_LORA, N_HEADS, HEAD_PAD - QK_NOPE), F32)],
                          axis=-1).reshape(KV_LORA, N_HEADS * HEAD_PAD)
    w_v = wkv[..., QK_NOPE:].reshape(KV_LORA, N_HEADS * V_HEAD)

    def block_diag(wb):
        eye = jnp.eye(LRU_BLOCKS, dtype=F32)
        return jnp.einsum('ncd,nm->ncmd', wb, eye).reshape(LRU_WIDTH, LRU_WIDTH)

    w_gates = jnp.concatenate([block_diag(w_rg[0]), block_diag(w_ig[0]),
                               block_diag(w_rg[1]), block_diag(w_ig[1])], axis=-1)
    b_gates = jnp.concatenate([b_rg[0], b_ig[0], b_rg[1], b_ig[1]]).reshape(1, -1)

    w_r = jnp.concatenate([w_router, w_group,
                           jnp.zeros((D_MODEL, LANES - N_EXPERTS - N_GROUPS), F32)], axis=-1)
    w_r_hi = w_r.astype(BF16)
    w_r_lo = (w_r - w_r_hi.astype(F32)).astype(BF16)
    b_r = jnp.concatenate([b_router, b_group,
                           jnp.zeros((LANES - N_EXPERTS - N_GROUPS,), F32)]).reshape(1, -1)

    return {
        "norm1": row(norm1), "w_in": w_in2.astype(BF16), "q_norm": row(q_norm),
        "w_q": w_q.astype(BF16), "kv_norm": row(kv_norm), "w_k": w_k.astype(BF16),
        "w_v": w_v.astype(BF16),
        "conv_w": conv_w, "conv_b": row(conv_b), "w_gates": w_gates.astype(BF16),
        "b_gates": b_gates, "lru_lambda": lru_lambda, "lru_out_norm": row(lru_out_norm),
        "att_out_norm": row(att_out_norm), "w_out": w_out.astype(BF16), "norm2": row(norm2),
        "w_route": jnp.concatenate([w_r_hi, w_r_lo], axis=-1), "b_route": b_r,
        "w_gu": jnp.concatenate([w_gate, w_up], axis=-1).astype(BF16),
        "w_down": w_down.astype(BF16), "final_norm": row(final_norm),
    }


def _tile(n, pref):
    return pref if n % pref == 0 else n


def _trunk(x, w):
    batch, seq, _ = x.shape
    t_tok = batch * seq
    x2 = x.reshape(t_tok, D_MODEL)
    wt = dict(w)
    wt["cq"], wt["sq"], wt["ck"], wt["sk"] = _rope_tables(seq)
    q, k, v, xbr, gbr = _proj_in(x2, seq, wt, _tile(seq, 512))
    att = _attention(q, k, v, batch, seq, _tile(seq, 256), 4)
    lru = _lru(xbr, gbr, wt, batch, seq, min(512, seq // 2))
    x1, t, route = _post(x2, att, lru, wt, _tile(t_tok, 512))
    y = _moe(t, route, x1, wt, _tile(t_tok, 1024), _tile(t_tok, 256), _tile(t_tok, 512))
    return y.reshape(batch, seq, D_MODEL)


def kernel(x_prompt, x_sample, norm1, w_in, q_norm, w_q_up, kv_norm, w_kv_up, conv_w, conv_b, w_rg, b_rg, w_ig, b_ig, lru_lambda, att_out_norm, lru_out_norm, w_out, norm2, w_group, b_group, w_router, b_router, w_gate, w_up, w_down, final_norm):
    w = _layout_weights(norm1[0], w_in[0], q_norm[0], w_q_up[0], kv_norm[0], w_kv_up[0], conv_w[0],
                        conv_b[0], w_rg[0], b_rg[0], w_ig[0], b_ig[0], lru_lambda[0],
                        att_out_norm[0], lru_out_norm[0], w_out[0], norm2[0], w_group[0],
                        b_group[0], w_router[0], b_router[0], w_gate[0], w_up[0], w_down[0],
                        final_norm)
    return (_trunk(x_prompt, w), _trunk(x_sample, w))
```

```python
import functools
import math

import jax
import jax.numpy as jnp
from jax import lax
from jax.experimental import pallas as pl
from jax.experimental.pallas import tpu as pltpu

F32 = jnp.float32
BF16 = jnp.bfloat16

D_MODEL = 1024
N_HEADS = 8
N_PAIRS = N_HEADS // 2
QK_NOPE = 64
QK_ROPE = 32
V_HEAD = 64
HEAD_PAD = 128
Q_LORA = 384
KV_LORA = 256
ATT_WIDTH = N_HEADS * V_HEAD
ROPE_BASE = 10000.0
LRU_WIDTH = 512
LRU_BLOCKS = 8
LRU_BLOCK = LRU_WIDTH // LRU_BLOCKS
CONV_W = 4
LRU_C = 8.0
N_GROUPS = 4
EXPERTS_PER_GROUP = 8
N_EXPERTS = N_GROUPS * EXPERTS_PER_GROUP
D_EXPERT = 256
EPS = 1e-6
LANES = 128

C_QLAT = 0
C_KVLAT = C_QLAT + Q_LORA
C_XBR = C_KVLAT + KV_LORA
C_GBR = C_XBR + LRU_WIDTH
C_KR = C_GBR + LRU_WIDTH
C_KRS = C_KR + HEAD_PAD
IN_PAD = C_KRS + HEAD_PAD

VMEM_LIMIT = 56 * 1024 * 1024


def _rms(x):
    return x * lax.rsqrt(jnp.mean(x * x, axis=-1, keepdims=True) + EPS)


def _sigmoid(x):
    return 1.0 / (1.0 + jnp.exp(-x))


def _sigmoid_tanh(x):
    return 0.5 * jnp.tanh(0.5 * x) + 0.5


def _gelu_tanh(x):
    return 0.5 * x * (1.0 + jnp.tanh(math.sqrt(2.0 / math.pi) * (x + 0.044715 * (x * x * x))))


def _params(sem):
    return pltpu.CompilerParams(dimension_semantics=sem, vmem_limit_bytes=VMEM_LIMIT)


SLAB = D_MODEL // LANES


def _store_slabs(ref, row0, x):
    n = x.shape[0]
    for s in range(SLAB):
        ref[pl.ds(row0 + s, n, stride=SLAB), :] = x[:, s * LANES:(s + 1) * LANES]


def _load_slabs(ref, row0, n):
    return jnp.concatenate([ref[pl.ds(row0 + s, n, stride=SLAB), :] for s in range(SLAB)], axis=-1)


def _proj_in_kernel(x_ref, n1_ref, win_ref, qn_ref, wq_ref, kvn_ref, wk_ref, wv_ref,
                    cq_ref, sq_ref, ck_ref, sk_ref,
                    q_ref, k_ref, v_ref, xbr_ref, gbr_ref):
    x = x_ref[...]
    u = _rms(x) * n1_ref[...]
    z = jnp.dot(u.astype(BF16), win_ref[...], preferred_element_type=F32)
    xbr_ref[...] = z[:, C_XBR:C_XBR + LRU_WIDTH].astype(BF16)
    gbr_ref[...] = z[:, C_GBR:C_GBR + LRU_WIDTH].astype(BF16)

    qn = (_rms(z[:, C_QLAT:C_QLAT + Q_LORA]) * qn_ref[...]).astype(BF16)
    q2 = jnp.dot(qn, wq_ref[...], preferred_element_type=F32)
    kvn = (_rms(z[:, C_KVLAT:C_KVLAT + KV_LORA]) * kvn_ref[...]).astype(BF16)
    kk = jnp.dot(kvn, wk_ref[...], preferred_element_type=F32)
    vv = jnp.dot(kvn, wv_ref[...], preferred_element_type=F32)

    cq, sq = cq_ref[...], sq_ref[...]
    kpe = z[:, C_KR:C_KR + HEAD_PAD] * ck_ref[...] + z[:, C_KRS:C_KRS + HEAD_PAD] * sk_ref[...]
    half = N_HEADS * HEAD_PAD
    for p in range(N_PAIRS):
        qs, ks = [], []
        for j in range(2):
            c0 = (2 * p + j) * HEAD_PAD
            qs.append(q2[:, c0:c0 + HEAD_PAD] * cq + q2[:, half + c0:half + c0 + HEAD_PAD] * sq)
            ks.append(kk[:, c0:c0 + HEAD_PAD] + kpe)
        q_ref[p] = jnp.concatenate(qs, axis=-1).astype(BF16)
        k_ref[p] = jnp.concatenate(ks, axis=-1).astype(BF16)
        v_ref[p] = vv[:, p * LANES:(p + 1) * LANES].astype(BF16)


def _proj_in(x2, seq, w, tm):
    t_tok = x2.shape[0]
    nt = t_tok // tm
    per_seq = seq // tm
    full = lambda a: pl.BlockSpec(a.shape, lambda i: (0,) * a.ndim)
    tab = pl.BlockSpec((tm, HEAD_PAD), lambda i: (i % per_seq, 0))
    pair_out = lambda width: pl.BlockSpec((N_PAIRS, tm, width), lambda i: (0, i, 0))
    row_out = pl.BlockSpec((tm, LRU_WIDTH), lambda i: (i, 0))
    return pl.pallas_call(
        _proj_in_kernel,
        grid=(nt,),
        in_specs=[pl.BlockSpec((tm, D_MODEL), lambda i: (i, 0)),
                  full(w["norm1"]), full(w["w_in"]), full(w["q_norm"]), full(w["w_q"]),
                  full(w["kv_norm"]), full(w["w_k"]), full(w["w_v"]), tab, tab, tab, tab],
        out_specs=[pair_out(2 * HEAD_PAD), pair_out(2 * HEAD_PAD), pair_out(LANES), row_out, row_out],
        out_shape=[jax.ShapeDtypeStruct((N_PAIRS, t_tok, 2 * HEAD_PAD), BF16),
                   jax.ShapeDtypeStruct((N_PAIRS, t_tok, 2 * HEAD_PAD), BF16),
                   jax.ShapeDtypeStruct((N_PAIRS, t_tok, LANES), BF16),
                   jax.ShapeDtypeStruct((t_tok, LRU_WIDTH), BF16),
                   jax.ShapeDtypeStruct((t_tok, LRU_WIDTH), BF16)],
        compiler_params=_params(("arbitrary",)),
        name="proj_in",
    )(x2, w["norm1"], w["w_in"], w["q_norm"], w["w_q"], w["kv_norm"], w["w_k"], w["w_v"],
      w["cq"], w["sq"], w["ck"], w["sk"])


def _attn_kernel(q_ref, k_ref, v_ref, o_ref, *, pairs):
    for pp in range(pairs):
        v = v_ref[pp]
        outs = []
        for j in range(2):
            q = q_ref[pp, :, j * HEAD_PAD:(j + 1) * HEAD_PAD]
            k = k_ref[pp, :, j * HEAD_PAD:(j + 1) * HEAD_PAD]
            s = lax.dot_general(q, k, (((1,), (1,)), ((), ())), preferred_element_type=F32)
            m = jnp.max(s, axis=-1, keepdims=True)
            p = jnp.exp2(s - m)
            l = jnp.sum(p, axis=-1, keepdims=True)
            o = jnp.dot(p.astype(BF16), v, preferred_element_type=F32)
            outs.append(o / l)
        lane = lax.broadcasted_iota(jnp.int32, outs[0].shape, 1)
        o_ref[:, pp * LANES:(pp + 1) * LANES] = jnp.where(lane < V_HEAD, outs[0],
                                                          outs[1]).astype(BF16)


def _attention(q, k, v, batch, seq, tq, pairs):
    t_tok = q.shape[1]
    nq = seq // tq
    return pl.pallas_call(
        functools.partial(_attn_kernel, pairs=pairs),
        grid=(batch, N_PAIRS // pairs, nq),
        in_specs=[pl.BlockSpec((pairs, tq, 2 * HEAD_PAD), lambda b, p, i: (p, b * nq + i, 0)),
                  pl.BlockSpec((pairs, seq, 2 * HEAD_PAD), lambda b, p, i: (p, b, 0)),
                  pl.BlockSpec((pairs, seq, LANES), lambda b, p, i: (p, b, 0))],
        out_specs=pl.BlockSpec((tq, pairs * LANES), lambda b, p, i: (b * nq + i, p)),
        out_shape=jax.ShapeDtypeStruct((t_tok, ATT_WIDTH), BF16),
        compiler_params=_params(("arbitrary", "arbitrary", "arbitrary")),
        name="attn",
    )(q, k, v)


def _lru_kernel(x_ref, g_ref, cw_ref, cb_ref, wg_ref, bg_ref, lam_ref, on_ref, o_ref,
                xconv, hsum, af, uf, ab, ub, *, seq, n):
    nc = seq // n
    assert nc % 2 == 0 and nc * n == seq
    halo = 16
    rows = n + 2 * halo

    def conv(ci, carry):
        t0 = pl.multiple_of(ci * n, n)
        lo = pl.multiple_of(jnp.maximum(t0 - halo, 0), halo)
        hi = pl.multiple_of(jnp.minimum(t0 + n, seq - halo), halo)
        before = jnp.where(ci > 0, x_ref[pl.ds(lo, halo), :].astype(F32), 0.0)
        after = jnp.where(ci < nc - 1, x_ref[pl.ds(hi, halo), :].astype(F32), 0.0)
        xw = jnp.concatenate([before, x_ref[pl.ds(t0, n), :].astype(F32), after], axis=0)
        xc = cb_ref[...] + cw_ref[0:1, :] * pltpu.roll(xw, 1, 0)[halo:halo + n]
        xc = xc + cw_ref[1:2, :] * xw[halo:halo + n]
        xc = xc + cw_ref[2:3, :] * pltpu.roll(xw, rows - 1, 0)[halo:halo + n]
        xc = xc + cw_ref[3:4, :] * pltpu.roll(xw, rows - 2, 0)[halo:halo + n]
        xconv[pl.ds(t0, n), :] = xc
        return carry

    lax.fori_loop(0, nc, conv, 0)

    lam = lam_ref[...]
    neg = -lam
    softplus = jnp.maximum(neg, 0.0) + jnp.log1p(jnp.exp(-jnp.abs(neg)))
    c_all = (-LRU_C * math.log2(math.e)) * softplus

    def coeffs(t0, d):
        xc = xconv[pl.ds(t0, n), :]
        w0 = 2 * LRU_WIDTH * d
        pre = jnp.dot(xc.astype(BF16), wg_ref[:, w0:w0 + 2 * LRU_WIDTH],
                      preferred_element_type=F32) + bg_ref[:, w0:w0 + 2 * LRU_WIDTH]
        r = _sigmoid_tanh(pre[:, :LRU_WIDTH])
        i = _sigmoid_tanh(pre[:, LRU_WIDTH:])
        a = jnp.exp2(c_all[d:d + 1, :] * r)
        u = jnp.exp2(0.5 * jnp.log2(1.0 - a * a)) * (i * xc)
        return a, u

    def pair(ci, carry, *, first):
        t0f = pl.multiple_of(ci * n, n)
        t0b = pl.multiple_of((nc - 1 - ci) * n, n)
        a, u = coeffs(t0f, 0)
        af[...] = a
        uf[...] = u
        a, u = coeffs(t0b, 1)
        ab[...] = a
        ub[...] = u

        def step(jj, c2):
            h1, h2 = c2
            f0 = pl.multiple_of(jj * 8, 8)
            b0 = pl.multiple_of(n - 8 - jj * 8, 8)
            a1, u1 = af.at[pl.ds(f0, 8), :], uf.at[pl.ds(f0, 8), :]
            a2, u2 = ab.at[pl.ds(b0, 8), :], ub.at[pl.ds(b0, 8), :]
            for k in range(8):
                h1 = a1[k:k + 1, :] * h1 + u1[k:k + 1, :]
                h2 = a2[7 - k:8 - k, :] * h2 + u2[7 - k:8 - k, :]
                u1[k:k + 1, :] = h1
                u2[7 - k:8 - k, :] = h2
            return h1, h2

        carry = lax.fori_loop(0, n // 8, step, carry)
        if first:
            hsum[pl.ds(t0f, n), :] = uf[...]
            hsum[pl.ds(t0b, n), :] = ub[...]
        else:
            hsum[pl.ds(t0f, n), :] += uf[...]
            hsum[pl.ds(t0b, n), :] += ub[...]
        return carry

    h0 = jnp.zeros((1, LRU_WIDTH), F32)
    carry = lax.fori_loop(0, nc // 2, functools.partial(pair, first=True), (h0, h0))
    lax.fori_loop(nc // 2, nc, functools.partial(pair, first=False), carry)

    def fin(ci, carry):
        t0 = pl.multiple_of(ci * n, n)
        h = hsum[pl.ds(t0, n), :]
        y = h * _gelu_tanh(g_ref[pl.ds(t0, n), :].astype(F32))
        o_ref[pl.ds(t0, n), :] = (_rms(y) * on_ref[...]).astype(BF16)
        return carry

    lax.fori_loop(0, nc, fin, 0)


def _lru(xbr, gbr, w, batch, seq, n):
    t_tok = xbr.shape[0]
    full = lambda a: pl.BlockSpec(a.shape, lambda b: (0,) * a.ndim)
    blk = pl.BlockSpec((seq, LRU_WIDTH), lambda b: (b, 0))
    return pl.pallas_call(
        functools.partial(_lru_kernel, seq=seq, n=n),
        grid=(batch,),
        in_specs=[blk, blk, full(w["conv_w"]), full(w["conv_b"]), full(w["w_gates"]),
                  full(w["b_gates"]), full(w["lru_lambda"]), full(w["lru_out_norm"])],
        out_specs=blk,
        out_shape=jax.ShapeDtypeStruct((t_tok, LRU_WIDTH), BF16),
        scratch_shapes=[pltpu.VMEM((seq, LRU_WIDTH), F32),
                        pltpu.VMEM((seq, LRU_WIDTH), F32),
                        pltpu.VMEM((n, LRU_WIDTH), F32),
                        pltpu.VMEM((n, LRU_WIDTH), F32),
                        pltpu.VMEM((n, LRU_WIDTH), F32),
                        pltpu.VMEM((n, LRU_WIDTH), F32)],
        compiler_params=_params(("arbitrary",)),
        name="lru",
    )(xbr, gbr, w["conv_w"], w["conv_b"], w["w_gates"], w["b_gates"], w["lru_lambda"],
      w["lru_out_norm"])


def _post_kernel(x_ref, att_ref, lru_ref, an_ref, wo_ref, n2_ref, wr_ref, br_ref,
                 x1_ref, t_ref, route_ref):
    att = att_ref[...].astype(F32)
    attn = (_rms(att) * an_ref[...]).astype(BF16)
    mixed = jnp.concatenate([attn, lru_ref[...]], axis=-1)
    x1 = x_ref[...] + jnp.dot(mixed, wo_ref[...], preferred_element_type=F32)
    x1_ref[...] = x1
    t = _rms(x1) * n2_ref[...]
    _store_slabs(t_ref, 0, t)
    t_hi = t.astype(BF16)
    t_lo = (t - t_hi.astype(F32)).astype(BF16)
    lg = (jnp.dot(t_hi, wr_ref[...], preferred_element_type=F32)
          + jnp.dot(t_lo, wr_ref[...], preferred_element_type=F32))
    logits = lg[:, :LANES] + lg[:, LANES:] + br_ref[...]

    lane = lax.broadcasted_iota(jnp.int32, logits.shape, 1)
    lanef = lane.astype(F32)
    big = jnp.float32(1e9)
    ninf = jnp.float32(-jnp.inf)
    gmask = (lane >= N_EXPERTS) & (lane < N_EXPERTS + N_GROUPS)
    gl = jnp.where(gmask, logits, ninf)
    gmax = jnp.max(gl, axis=-1, keepdims=True)
    gsel = jnp.min(jnp.where(gl == gmax, lanef, big), axis=-1, keepdims=True) - N_EXPERTS
    gw = 1.0 / jnp.sum(jnp.where(gmask, jnp.exp(gl - gmax), 0.0), axis=-1, keepdims=True)
    lane_group = (lane // EXPERTS_PER_GROUP).astype(F32)
    emask = (lane < N_EXPERTS) & (lane_group == gsel)
    el = jnp.where(emask, logits, ninf)
    v1 = jnp.max(el, axis=-1, keepdims=True)
    i1 = jnp.min(jnp.where(el == v1, lanef, big), axis=-1, keepdims=True)
    el2 = jnp.where(lanef == i1, ninf, el)
    v2 = jnp.max(el2, axis=-1, keepdims=True)
    i2 = jnp.min(jnp.where(el2 == v2, lanef, big), axis=-1, keepdims=True)
    e2 = jnp.exp(v2 - v1)
    w1 = (1.0 / (1.0 + e2)) * gw
    w2 = (e2 / (1.0 + e2)) * gw
    route_ref[...] = (jnp.where(lane == 0, i1, 0.0) + jnp.where(lane == 1, i2, 0.0)
                      + jnp.where(lane == 2, w1, 0.0) + jnp.where(lane == 3, w2, 0.0))


def _post(x2, att, lru, w, tm):
    t_tok = x2.shape[0]
    full = lambda a: pl.BlockSpec(a.shape, lambda i: (0,) * a.ndim)
    row = lambda width: pl.BlockSpec((tm, width), lambda i: (i, 0))
    return pl.pallas_call(
        _post_kernel,
        grid=(t_tok // tm,),
        in_specs=[row(D_MODEL), row(ATT_WIDTH), row(LRU_WIDTH), full(w["att_out_norm"]),
                  full(w["w_out"]), full(w["norm2"]), full(w["w_route"]), full(w["b_route"])],
        out_specs=[row(D_MODEL), pl.BlockSpec((tm * SLAB, LANES), lambda i: (i, 0)), row(LANES)],
        out_shape=[jax.ShapeDtypeStruct((t_tok, D_MODEL), F32),
                   jax.ShapeDtypeStruct((t_tok * SLAB, LANES), F32),
                   jax.ShapeDtypeStruct((t_tok, LANES), F32)],
        compiler_params=_params(("arbitrary",)),
        name="post",
    )(x2, att, lru, w["att_out_norm"], w["w_out"], w["norm2"], w["w_route"], w["b_route"])


def _route_plan(route, tr):
    t_tok = route.shape[0]
    ids = route[:, 0:2].astype(jnp.int32)
    flat = ids.T.reshape(-1)
    onehot = (flat[:, None] == jnp.arange(N_EXPERTS, dtype=jnp.int32)[None, :]).astype(jnp.int32)
    incl = jnp.cumsum(onehot, axis=0)
    counts = incl[-1]
    rank = jnp.sum((incl - onehot) * onehot, axis=-1)
    padded = ((counts + tr - 1) // tr) * tr
    ends = jnp.cumsum(padded)
    starts = ends - padded
    pos = (starts[flat] + rank).astype(jnp.int32).reshape(2, t_tok)
    zero_start = jnp.where(padded > 0, ends - tr, -1).astype(jnp.int32)
    n_tiles = (2 * t_tok) // tr + N_EXPERTS
    tile_first = jnp.arange(n_tiles, dtype=jnp.int32) * tr
    tile_expert = jnp.minimum(jnp.sum((ends[None, :] <= tile_first[:, None]).astype(jnp.int32), axis=-1),
                              N_EXPERTS - 1).astype(jnp.int32)
    n_used = (ends[-1] // tr).astype(jnp.int32).reshape(1)
    return pos, zero_start, tile_expert, n_used


def _tile_major(pos, tm):
    return pos.reshape(2, -1, tm).transpose(1, 0, 2).reshape(-1)


def _row_copy(src, src_row, dst, dst_row, sem):
    s0 = pl.multiple_of(src_row * SLAB, SLAB)
    d0 = pl.multiple_of(dst_row * SLAB, SLAB)
    return pltpu.make_async_copy(src.at[pl.ds(s0, SLAB)], dst.at[pl.ds(d0, SLAB)], sem)


def _dispatch_kernel(zs_ref, pos_ref, t_ref, xs_hbm, zbuf, sem, zsem, *, tm, tr):
    i = pl.program_id(0)

    def zero_copy(e):
        z0 = pl.multiple_of(zs_ref[e] * SLAB, tr * SLAB)
        return pltpu.make_async_copy(zbuf, xs_hbm.at[pl.ds(z0, tr * SLAB)], zsem)

    @pl.when(i == 0)
    def _():
        zbuf[...] = jnp.zeros_like(zbuf)
        for e in range(N_EXPERTS):
            @pl.when(zs_ref[e] >= 0)
            def _():
                zero_copy(e).start()
        for e in range(N_EXPERTS):
            @pl.when(zs_ref[e] >= 0)
            def _():
                zero_copy(e).wait()

    def issue(j, carry):
        _row_copy(t_ref, j, xs_hbm, pos_ref[j], sem).start(priority=0)
        _row_copy(t_ref, j, xs_hbm, pos_ref[tm + j], sem).start(priority=1)
        return carry

    lax.fori_loop(0, tm, issue, 0, unroll=8)

    def drain(j, carry):
        _row_copy(t_ref, 0, xs_hbm, 0, sem).wait()
        _row_copy(t_ref, 0, xs_hbm, 0, sem).wait()
        return carry

    lax.fori_loop(0, tm, drain, 0, unroll=8)


def _dispatch(t, pos_tiles, zero_start, n_rows, tm, tr):
    t_tok = t.shape[0] // SLAB
    return pl.pallas_call(
        functools.partial(_dispatch_kernel, tm=tm, tr=tr),
        grid_spec=pltpu.PrefetchScalarGridSpec(
            num_scalar_prefetch=1,
            grid=(t_tok // tm,),
            in_specs=[pl.BlockSpec((2 * tm,), lambda i, zs: (i,), memory_space=pltpu.SMEM),
                      pl.BlockSpec((tm * SLAB, LANES), lambda i, zs: (i, 0))],
            out_specs=pl.BlockSpec(memory_space=pl.ANY),
            scratch_shapes=[pltpu.VMEM((tr * SLAB, LANES), F32), pltpu.SemaphoreType.DMA(()),
                            pltpu.SemaphoreType.DMA(())]),
        out_shape=jax.ShapeDtypeStruct((n_rows * SLAB, LANES), F32),
        compiler_params=_params(("arbitrary",)),
        name="dispatch",
    )(zero_start, pos_tiles, t)


def _experts_kernel(te_ref, nu_ref, xs_ref, wgu_ref, wd_ref, ys_ref, *, tr):
    @pl.when(pl.program_id(0) < nu_ref[0])
    def _():
        x = _load_slabs(xs_ref, 0, tr).astype(BF16)
        gu = jnp.dot(x, wgu_ref[0], preferred_element_type=F32)
        a = gu[:, :D_EXPERT]
        h = (a * _sigmoid(a)) * gu[:, D_EXPERT:]
        _store_slabs(ys_ref, 0, jnp.dot(h.astype(BF16), wd_ref[0], preferred_element_type=F32))


def _experts(xs, tile_expert, n_used, w, tr):
    n_rows = xs.shape[0] // SLAB
    row = pl.BlockSpec((tr * SLAB, LANES), lambda i, te, nu: (jnp.minimum(i, nu[0] - 1), 0))
    return pl.pallas_call(
        functools.partial(_experts_kernel, tr=tr),
        grid_spec=pltpu.PrefetchScalarGridSpec(
            num_scalar_prefetch=2,
            grid=(n_rows // tr,),
            in_specs=[row,
                      pl.BlockSpec((1, D_MODEL, 2 * D_EXPERT), lambda i, te, nu: (te[i], 0, 0)),
                      pl.BlockSpec((1, D_EXPERT, D_MODEL), lambda i, te, nu: (te[i], 0, 0))],
            out_specs=row),
        out_shape=jax.ShapeDtypeStruct((n_rows * SLAB, LANES), F32),
        compiler_params=_params(("arbitrary",)),
        name="experts",
    )(tile_expert, n_used, xs, w["w_gu"], w["w_down"])


def _combine_kernel(pos_ref, nxt_ref, x1_ref, route_ref, fn_ref, ys_hbm, o_ref, ybuf, sem, *, tm):
    i = pl.program_id(0)
    n = pl.num_programs(0)
    slot = i % 2

    def issue(p_ref, s):
        def body(j, carry):
            _row_copy(ys_hbm, p_ref[j], ybuf.at[s], j, sem.at[s]).start(priority=0)
            _row_copy(ys_hbm, p_ref[tm + j], ybuf.at[s], tm + j, sem.at[s]).start(priority=1)
            return carry
        lax.fori_loop(0, tm, body, 0, unroll=8)

    @pl.when(i == 0)
    def _():
        issue(pos_ref, 0)

    @pl.when(i + 1 < n)
    def _():
        issue(nxt_ref, 1 - slot)

    def drain(j, carry):
        _row_copy(ys_hbm, 0, ybuf.at[slot], 0, sem.at[slot]).wait()
        _row_copy(ys_hbm, 0, ybuf.at[slot], 0, sem.at[slot]).wait()
        return carry

    lax.fori_loop(0, tm, drain, 0, unroll=8)

    r = route_ref[...]
    yb = ybuf.at[slot]
    moe = r[:, 2:3] * _load_slabs(yb, 0, tm) + r[:, 3:4] * _load_slabs(yb, tm * SLAB, tm)
    o_ref[...] = _rms(x1_ref[...] + moe) * fn_ref[...]


def _combine(ys, pos_tiles, x1, route, w, tm):
    t_tok = x1.shape[0]
    nt = t_tok // tm
    row = lambda width: pl.BlockSpec((tm, width), lambda i: (i, 0))
    return pl.pallas_call(
        functools.partial(_combine_kernel, tm=tm),
        grid=(nt,),
        in_specs=[pl.BlockSpec((2 * tm,), lambda i: (i,), memory_space=pltpu.SMEM),
                  pl.BlockSpec((2 * tm,), lambda i: (jnp.minimum(i + 1, nt - 1),),
                               memory_space=pltpu.SMEM),
                  row(D_MODEL), row(LANES), pl.BlockSpec((1, D_MODEL), lambda i: (0, 0)),
                  pl.BlockSpec(memory_space=pl.ANY)],
        out_specs=row(D_MODEL),
        out_shape=jax.ShapeDtypeStruct((t_tok, D_MODEL), F32),
        scratch_shapes=[pltpu.VMEM((2, 2 * tm * SLAB, LANES), F32),
                        pltpu.SemaphoreType.DMA((2,))],
        compiler_params=_params(("arbitrary",)),
        name="combine",
    )(pos_tiles, pos_tiles, x1, route, w["final_norm"], ys)


def _moe(t, route, x1, w, tm_dispatch, tm_combine, tr):
    t_tok = x1.shape[0]
    n_rows = 2 * t_tok + N_EXPERTS * tr
    pos, zero_start, tile_expert, n_used = _route_plan(route, tr)
    xs = _dispatch(t, _tile_major(pos, tm_dispatch), zero_start, n_rows, tm_dispatch, tr)
    ys = _experts(xs, tile_expert, n_used, w, tr)
    return _combine(ys, _tile_major(pos, tm_combine), x1, route, w, tm_combine)


def _rope_tables(seq):
    pos = jnp.arange(seq, dtype=F32)
    inv_freq = 1.0 / (ROPE_BASE ** (jnp.arange(0, QK_ROPE, 2, dtype=F32) / QK_ROPE))
    freqs = pos[:, None] * inv_freq[None, :]
    cos2 = jnp.concatenate([jnp.cos(freqs), jnp.cos(freqs)], axis=-1)
    sin2 = jnp.concatenate([jnp.sin(freqs), jnp.sin(freqs)], axis=-1)
    scale = (QK_NOPE + QK_ROPE) ** -0.5 * math.log2(math.e)
    pad =jnp.zeros((seq, HEAD_PAD - QK_NOPE - QK_ROPE), F32)
    ones = jnp.ones((seq, QK_NOPE), F32)
    zeros = jnp.zeros((seq, QK_NOPE), F32)
    cq = jnp.concatenate([ones, cos2, pad], axis=-1) * scale
    sq = jnp.concatenate([zeros, sin2, pad], axis=-1) * scale
    ck = jnp.concatenate([zeros, cos2, pad], axis=-1)
    sk = jnp.concatenate([zeros, sin2, pad], axis=-1)
    return cq, sq, ck, sk


def _rot_half_cols(wp):
    hr = QK_ROPE // 2
    return jnp.concatenate([-wp[..., hr:], wp[..., :hr]], axis=-1)


def _layout_weights(norm1, w_in, q_norm, w_q_up, kv_norm, w_kv_up, conv_w, conv_b, w_rg, b_rg,
                    w_ig, b_ig, lru_lambda, att_out_norm, lru_out_norm, w_out, norm2, w_group,
                    b_group, w_router, b_router, w_gate, w_up, w_down, final_norm):
    row = lambda a: a.reshape(1, -1)
    c1 = Q_LORA
    c2 = c1 + KV_LORA
    c3 = c2 + QK_ROPE
    c4 = c3 + LRU_WIDTH
    w_kr = w_in[:, c2:c3]
    zl = jnp.zeros((D_MODEL, QK_NOPE), F32)
    zr = jnp.zeros((D_MODEL, HEAD_PAD - QK_NOPE - QK_ROPE), F32)
    w_in2 = jnp.concatenate([w_in[:, :c2], w_in[:, c3:c4], w_in[:, c4:],
                             zl, w_kr, zr, zl, _rot_half_cols(w_kr), zr], axis=-1)

    wq = w_q_up.reshape(Q_LORA, N_HEADS, QK_NOPE + QK_ROPE)
    zq = jnp.zeros((Q_LORA, N_HEADS, HEAD_PAD - QK_NOPE - QK_ROPE), F32)
    wq_plain = jnp.concatenate([wq, zq], axis=-1).reshape(Q_LORA, N_HEADS * HEAD_PAD)
    wq_rot = jnp.concatenate([jnp.zeros((Q_LORA, N_HEADS, QK_NOPE), F32),
                              _rot_half_cols(wq[..., QK_NOPE:]), zq], axis=-1)
    w_q = jnp.concatenate([wq_plain, wq_rot.reshape(Q_LORA, N_HEADS * HEAD_PAD)], axis=-1)

    wkv = w_kv_up.reshape(KV_LORA, N_HEADS, QK_NOPE + V_HEAD)
    w_k = jnp.concatenate([wkv[..., :QK_NOPE],
                           jnp.zeros((KV_LORA, N_HEADS, HEAD_PAD - QK_NOPE), F32)],
                          axis=-1).reshape(KV_LORA, N_HEADS * HEAD_PAD)
    w_v = wkv[..., QK_NOPE:].reshape(KV_LORA, N_HEADS * V_HEAD)

    def block_diag(wb):
        eye = jnp.eye(LRU_BLOCKS, dtype=F32)
        return jnp.einsum('ncd,nm->ncmd', wb, eye).reshape(LRU_WIDTH, LRU_WIDTH)

    w_gates = jnp.concatenate([block_diag(w_rg[0]), block_diag(w_ig[0]),
                               block_diag(w_rg[1]), block_diag(w_ig[1])], axis=-1)
    b_gates = jnp.concatenate([b_rg[0], b_ig[0], b_rg[1], b_ig[1]]).reshape(1, -1)

    w_r = jnp.concatenate([w_router, w_group,
                           jnp.zeros((D_MODEL, LANES - N_EXPERTS - N_GROUPS), F32)], axis=-1)
    w_r_hi = w_r.astype(BF16)
    w_r_lo = (w_r - w_r_hi.astype(F32)).astype(BF16)
    b_r = jnp.concatenate([b_router, b_group,
                           jnp.zeros((LANES - N_EXPERTS - N_GROUPS,), F32)]).reshape(1, -1)

    return {
        "norm1": row(norm1), "w_in": w_in2.astype(BF16), "q_norm": row(q_norm),
        "w_q": w_q.astype(BF16), "kv_norm": row(kv_norm), "w_k": w_k.astype(BF16),
        "w_v": w_v.astype(BF16),
        "conv_w": conv_w, "conv_b": row(conv_b), "w_gates": w_gates.astype(BF16),
        "b_gates": b_gates, "lru_lambda": lru_lambda, "lru_out_norm": row(lru_out_norm),
        "att_out_norm": row(att_out_norm), "w_out": w_out.astype(BF16), "norm2": row(norm2),
        "w_route": jnp.concatenate([w_r_hi, w_r_lo], axis=-1), "b_route": b_r,
        "w_gu": jnp.concatenate([w_gate, w_up], axis=-1).astype(BF16),
        "w_down": w_down.astype(BF16), "final_norm": row(final_norm),
    }


def _tile(n, pref):
    return pref if n % pref == 0 else n


def _trunk(x, w):
    batch, seq, _ = x.shape
    t_tok = batch * seq
    x2 = x.reshape(t_tok, D_MODEL)
    wt = dict(w)
    wt["cq"], wt["sq"], wt["ck"], wt["sk"] = _rope_tables(seq)
    q, k, v, xbr, gbr = _proj_in(x2, seq, wt, _tile(seq, 512))
    att = _attention(q, k, v, batch, seq, _tile(seq, 256), 4)
    lru = _lru(xbr, gbr, wt, batch, seq, min(512, seq // 2))
    x1, t, route = _post(x2, att, lru, wt, _tile(t_tok, 512))
    y = _moe(t, route, x1, wt, _tile(t_tok, 1024), _tile(t_tok, 256), _tile(t_tok, 512))
    return y.reshape(batch, seq, D_MODEL)


def kernel(x_prompt, x_sample, norm1, w_in, q_norm, w_q_up, kv_norm, w_kv_up, conv_w, conv_b, w_rg, b_rg, w_ig, b_ig, lru_lambda, att_out_norm, lru_out_norm, w_out, norm2, w_group, b_group, w_router, b_router, w_gate, w_up, w_down, final_norm):
    w = _layout_weights(norm1[0], w_in[0], q_norm[0], w_q_up[0], kv_norm[0], w_kv_up[0], conv_w[0],
                        conv_b[0], w_rg[0], b_rg[0], w_ig[0], b_ig[0], lru_lambda[0],
                        att_out_norm[0], lru_out_norm[0], w_out[0], norm2[0], w_group[0],
                        b_group[0], w_router[0], b_router[0], w_gate[0], w_up[0], w_down[0],
                        final_norm)
    return (_trunk(x_prompt, w), _trunk(x_sample, w))
```

```python
import functools
import math

import jax
import jax.numpy as jnp
from jax import lax
from jax.experimental import pallas as pl
from jax.experimental.pallas import tpu as pltpu

F32 = jnp.float32
BF16 = jnp.bfloat16

D_MODEL = 1024
N_HEADS = 8
N_PAIRS = N_HEADS // 2
QK_NOPE = 64
QK_ROPE = 32
V_HEAD = 64
HEAD_PAD = 128
Q_LORA = 384
KV_LORA = 256
ATT_WIDTH = N_HEADS * V_HEAD
ROPE_BASE = 10000.0
LRU_WIDTH = 512
LRU_BLOCKS = 8
LRU_BLOCK = LRU_WIDTH // LRU_BLOCKS
CONV_W = 4
LRU_C = 8.0
N_GROUPS = 4
EXPERTS_PER_GROUP = 8
N_EXPERTS = N_GROUPS * EXPERTS_PER_GROUP
D_EXPERT = 256
EPS = 1e-6
LANES = 128

C_QLAT = 0
C_KVLAT = C_QLAT + Q_LORA
C_XBR = C_KVLAT + KV_LORA
C_GBR = C_XBR + LRU_WIDTH
C_KR = C_GBR + LRU_WIDTH
C_KRS = C_KR + HEAD_PAD
IN_PAD = C_KRS + HEAD_PAD

VMEM_LIMIT = 56 * 1024 * 1024


def _rms(x):
    return x * lax.rsqrt(jnp.mean(x * x, axis=-1, keepdims=True) + EPS)


def _sigmoid(x):
    return 1.0 / (1.0 + jnp.exp(-x))


def _sigmoid_tanh(x):
    return 0.5 * jnp.tanh(0.5 * x) + 0.5


def _gelu_tanh(x):
    return 0.5 * x * (1.0 + jnp.tanh(math.sqrt(2.0 / math.pi) * (x + 0.044715 * (x * x * x))))


def _params(sem):
    return pltpu.CompilerParams(dimension_semantics=sem, vmem_limit_bytes=VMEM_LIMIT)


SLAB = D_MODEL // LANES


def _store_slabs(ref, row0, x):
    n = x.shape[0]
    for s in range(SLAB):
        ref[pl.ds(row0 + s, n, stride=SLAB), :] = x[:, s * LANES:(s + 1) * LANES]


def _load_slabs(ref, row0, n):
    return jnp.concatenate([ref[pl.ds(row0 + s, n, stride=SLAB), :] for s in range(SLAB)], axis=-1)


def _proj_in_kernel(x_ref, n1_ref, win_ref, qn_ref, wq_ref, kvn_ref, wk_ref, wv_ref,
                    cq_ref, sq_ref, ck_ref, sk_ref,
                    q_ref, k_ref, v_ref, xbr_ref, gbr_ref):
    x = x_ref[...]
    u = _rms(x) * n1_ref[...]
    z = jnp.dot(u.astype(BF16), win_ref[...], preferred_element_type=F32)
    xbr_ref[...] = z[:, C_XBR:C_XBR + LRU_WIDTH].astype(BF16)
    gbr_ref[...] = z[:, C_GBR:C_GBR + LRU_WIDTH].astype(BF16)

    qn = (_rms(z[:, C_QLAT:C_QLAT + Q_LORA]) * qn_ref[...]).astype(BF16)
    q2 = jnp.dot(qn, wq_ref[...], preferred_element_type=F32)
    kvn = (_rms(z[:, C_KVLAT:C_KVLAT + KV_LORA]) * kvn_ref[...]).astype(BF16)
    kk = jnp.dot(kvn, wk_ref[...], preferred_element_type=F32)
    vv = jnp.dot(kvn, wv_ref[...], preferred_element_type=F32)

    cq, sq = cq_ref[...], sq_ref[...]
    kpe = z[:, C_KR:C_KR + HEAD_PAD] * ck_ref[...] + z[:, C_KRS:C_KRS + HEAD_PAD] * sk_ref[...]
    half = N_HEADS * HEAD_PAD
    for p in range(N_PAIRS):
        qs, ks = [], []
        for j in range(2):
            c0 = (2 * p + j) * HEAD_PAD
            qs.append(q2[:, c0:c0 + HEAD_PAD] * cq + q2[:, half + c0:half + c0 + HEAD_PAD] * sq)
            ks.append(kk[:, c0:c0 + HEAD_PAD] + kpe)
        q_ref[p] = jnp.concatenate(qs, axis=-1).astype(BF16)
        k_ref[p] = jnp.concatenate(ks, axis=-1).astype(BF16)
        v_ref[p] = vv[:, p * LANES:(p + 1) * LANES].astype(BF16)


def _proj_in(x2, seq, w, tm):
    t_tok = x2.shape[0]
    nt = t_tok // tm
    per_seq = seq // tm
    full = lambda a: pl.BlockSpec(a.shape, lambda i: (0,) * a.ndim)
    tab = pl.BlockSpec((tm, HEAD_PAD), lambda i: (i % per_seq, 0))
    pair_out = lambda width: pl.BlockSpec((N_PAIRS, tm, width), lambda i: (0, i, 0))
    row_out = pl.BlockSpec((tm, LRU_WIDTH), lambda i: (i, 0))
    return pl.pallas_call(
        _proj_in_kernel,
        grid=(nt,),
        in_specs=[pl.BlockSpec((tm, D_MODEL), lambda i: (i, 0)),
                  full(w["norm1"]), full(w["w_in"]), full(w["q_norm"]), full(w["w_q"]),
                  full(w["kv_norm"]), full(w["w_k"]), full(w["w_v"]), tab, tab, tab, tab],
        out_specs=[pair_out(2 * HEAD_PAD), pair_out(2 * HEAD_PAD), pair_out(LANES), row_out, row_out],
        out_shape=[jax.ShapeDtypeStruct((N_PAIRS, t_tok, 2 * HEAD_PAD), BF16),
                   jax.ShapeDtypeStruct((N_PAIRS, t_tok, 2 * HEAD_PAD), BF16),
                   jax.ShapeDtypeStruct((N_PAIRS, t_tok, LANES), BF16),
                   jax.ShapeDtypeStruct((t_tok, LRU_WIDTH), BF16),
                   jax.ShapeDtypeStruct((t_tok, LRU_WIDTH), BF16)],
        compiler_params=_params(("arbitrary",)),
        name="proj_in",
    )(x2, w["norm1"], w["w_in"], w["q_norm"], w["w_q"], w["kv_norm"], w["w_k"], w["w_v"],
      w["cq"], w["sq"], w["ck"], w["sk"])


def _attn_kernel(q_ref, k_ref, v_ref, o_ref, *, pairs):
    for pp in range(pairs):
        v = v_ref[pp]
        outs = []
        for j in range(2):
            q = q_ref[pp, :, j * HEAD_PAD:(j + 1) * HEAD_PAD]
            k = k_ref[pp, :, j * HEAD_PAD:(j + 1) * HEAD_PAD]
            s = lax.dot_general(q, k, (((1,), (1,)), ((), ())), preferred_element_type=F32)
            m = jnp.max(s, axis=-1, keepdims=True)
            p = jnp.exp2(s - m)
            l = jnp.sum(p, axis=-1, keepdims=True)
            o = jnp.dot(p.astype(BF16), v, preferred_element_type=F32)
            outs.append(o / l)
        lane = lax.broadcasted_iota(jnp.int32, outs[0].shape, 1)
        o_ref[:, pp * LANES:(pp + 1) * LANES] = jnp.where(lane < V_HEAD, outs[0],
                                                          outs[1]).astype(BF16)


def _attention(q, k, v, batch, seq, tq, pairs):
    t_tok = q.shape[1]
    nq = seq // tq
    return pl.pallas_call(
        functools.partial(_attn_kernel, pairs=pairs),
        grid=(batch, N_PAIRS // pairs, nq),
        in_specs=[pl.BlockSpec((pairs, tq, 2 * HEAD_PAD), lambda b, p, i: (p, b * nq + i, 0)),
                  pl.BlockSpec((pairs, seq, 2 * HEAD_PAD), lambda b, p, i: (p, b, 0)),
                  pl.BlockSpec((pairs, seq, LANES), lambda b, p, i: (p, b, 0))],
        out_specs=pl.BlockSpec((tq, pairs * LANES), lambda b, p, i: (b * nq + i, p)),
        out_shape=jax.ShapeDtypeStruct((t_tok, ATT_WIDTH), BF16),
        compiler_params=_params(("arbitrary", "arbitrary", "arbitrary")),
        name="attn",
    )(q, k, v)


def _lru_kernel(x_ref, g_ref, cw_ref, cb_ref, wg_ref, bg_ref, lam_ref, on_ref, o_ref,
                xconv, hsum, af, uf, ab, ub, *, seq, n):
    nc = seq // n
    assert nc % 2 == 0 and nc * n == seq
    halo = 16
    rows = n + 2 * halo

    def conv(ci, carry):
        t0 = pl.multiple_of(ci * n, n)
        lo = pl.multiple_of(jnp.maximum(t0 - halo, 0), halo)
        hi = pl.multiple_of(jnp.minimum(t0 + n, seq - halo), halo)
        before = jnp.where(ci > 0, x_ref[pl.ds(lo, halo), :].astype(F32), 0.0)
        after = jnp.where(ci < nc - 1, x_ref[pl.ds(hi, halo), :].astype(F32), 0.0)
        xw = jnp.concatenate([before, x_ref[pl.ds(t0, n), :].astype(F32), after], axis=0)
        xc = cb_ref[...] + cw_ref[0:1, :] * pltpu.roll(xw, 1, 0)[halo:halo + n]
        xc = xc + cw_ref[1:2, :] * xw[halo:halo + n]
        xc = xc + cw_ref[2:3, :] * pltpu.roll(xw, rows - 1, 0)[halo:halo + n]
        xc = xc + cw_ref[3:4, :] * pltpu.roll(xw, rows - 2, 0)[halo:halo + n]
        xconv[pl.ds(t0, n), :] = xc
        return carry

    lax.fori_loop(0, nc, conv, 0)

    lam = lam_ref[...]
    neg = -lam
    softplus = jnp.maximum(neg, 0.0) + jnp.log1p(jnp.exp(-jnp.abs(neg)))
    c_all = (-LRU_C * math.log2(math.e)) * softplus

    def coeffs(t0, d):
        xc = xconv[pl.ds(t0, n), :]
        w0 = 2 * LRU_WIDTH * d
        pre = jnp.dot(xc.astype(BF16), wg_ref[:, w0:w0 + 2 * LRU_WIDTH],
                      preferred_element_type=F32) + bg_ref[:, w0:w0 + 2 * LRU_WIDTH]
        r = _sigmoid_tanh(pre[:, :LRU_WIDTH])
        i = _sigmoid_tanh(pre[:, LRU_WIDTH:])
        a = jnp.exp2(c_all[d:d + 1, :] * r)
        u = jnp.exp2(0.5 * jnp.log2(1.0 - a * a)) * (i * xc)
        return a, u

    def pair(ci, carry, *, first):
        t0f = pl.multiple_of(ci * n, n)
        t0b = pl.multiple_of((nc - 1 - ci) * n, n)
        a, u = coeffs(t0f, 0)
        af[...] = a
        uf[...] = u
        a, u = coeffs(t0b, 1)
        ab[...] = a
        ub[...] = u

        def step(jj, c2):
            h1, h2 = c2
            f0 = pl.multiple_of(jj * 8, 8)
            b0 = pl.multiple_of(n - 8 - jj * 8, 8)
            a1, u1 = af.at[pl.ds(f0, 8), :], uf.at[pl.ds(f0, 8), :]
            a2, u2 = ab.at[pl.ds(b0, 8), :], ub.at[pl.ds(b0, 8), :]
            for k in range(8):
                h1 = a1[k:k + 1, :] * h1 + u1[k:k + 1, :]
                h2 = a2[7 - k:8 - k, :] * h2 + u2[7 - k:8 - k, :]
                u1[k:k + 1, :] = h1
                u2[7 - k:8 - k, :] = h2
            return h1, h2

        carry = lax.fori_loop(0, n // 8, step, carry)
        if first:
            hsum[pl.ds(t0f, n), :] = uf[...]
            hsum[pl.ds(t0b, n), :] = ub[...]
        else:
            hsum[pl.ds(t0f, n), :] += uf[...]
            hsum[pl.ds(t0b, n), :] += ub[...]
        return carry

    h0 = jnp.zeros((1, LRU_WIDTH), F32)
    carry = lax.fori_loop(0, nc // 2, functools.partial(pair, first=True), (h0, h0))
    lax.fori_loop(nc // 2, nc, functools.partial(pair, first=False), carry)

    def fin(ci, carry):
        t0 = pl.multiple_of(ci * n, n)
        h = hsum[pl.ds(t0, n), :]
        y = h * _gelu_tanh(g_ref[pl.ds(t0, n), :].astype(F32))
        o_ref[pl.ds(t0, n), :] = (_rms(y) * on_ref[...]).astype(BF16)
        return carry

    lax.fori_loop(0, nc, fin, 0)


def _lru(xbr, gbr, w, batch, seq, n):
    t_tok = xbr.shape[0]
    full = lambda a: pl.BlockSpec(a.shape, lambda b: (0,) * a.ndim)
    blk = pl.BlockSpec((seq, LRU_WIDTH), lambda b: (b, 0))
    return pl.pallas_call(
        functools.partial(_lru_kernel, seq=seq, n=n),
        grid=(batch,),
        in_specs=[blk, blk, full(w["conv_w"]), full(w["conv_b"]), full(w["w_gates"]),
                  full(w["b_gates"]), full(w["lru_lambda"]), full(w["lru_out_norm"])],
        out_specs=blk,
        out_shape=jax.ShapeDtypeStruct((t_tok, LRU_WIDTH), BF16),
        scratch_shapes=[pltpu.VMEM((seq, LRU_WIDTH), F32),
                        pltpu.VMEM((seq, LRU_WIDTH), F32),
                        pltpu.VMEM((n, LRU_WIDTH), F32),
                        pltpu.VMEM((n, LRU_WIDTH), F32),
                        pltpu.VMEM((n, LRU_WIDTH), F32),
                        pltpu.VMEM((n, LRU_WIDTH), F32)],
        compiler_params=_params(("arbitrary",)),
        name="lru",
    )(xbr, gbr, w["conv_w"], w["conv_b"], w["w_gates"], w["b_gates"], w["lru_lambda"],
      w["lru_out_norm"])


def _post_kernel(x_ref, att_ref, lru_ref, an_ref, wo_ref, n2_ref, wr_ref, br_ref, tri_ref,
                 x1_ref, t_ref, route_ref, cnt_ref):
    att = att_ref[...].astype(F32)
    attn = (_rms(att) * an_ref[...]).astype(BF16)
    mixed = jnp.concatenate([attn, lru_ref[...]], axis=-1)
    x1 = x_ref[...] + jnp.dot(mixed, wo_ref[...], preferred_element_type=F32)
    x1_ref[...] = x1
    t = _rms(x1) * n2_ref[...]
    _store_slabs(t_ref, 0, t)
    t_hi = t.astype(BF16)
    t_lo = (t - t_hi.astype(F32)).astype(BF16)
    lg = (jnp.dot(t_hi, wr_ref[...], preferred_element_type=F32)
          + jnp.dot(t_lo, wr_ref[...], preferred_element_type=F32))
    logits = lg[:, :LANES] + lg[:, LANES:] + br_ref[...]

    lane = lax.broadcasted_iota(jnp.int32, logits.shape, 1)
    lanef = lane.astype(F32)
    big = jnp.float32(1e9)
    ninf = jnp.float32(-jnp.inf)
    gmask = (lane >= N_EXPERTS) & (lane < N_EXPERTS + N_GROUPS)
    gl = jnp.where(gmask, logits, ninf)
    gmax = jnp.max(gl, axis=-1, keepdims=True)
    gsel = jnp.min(jnp.where(gl == gmax, lanef, big), axis=-1, keepdims=True) - N_EXPERTS
    gw = 1.0 / jnp.sum(jnp.where(gmask, jnp.exp(gl - gmax), 0.0), axis=-1, keepdims=True)
    lane_group = (lane // EXPERTS_PER_GROUP).astype(F32)
    emask = (lane < N_EXPERTS) & (lane_group == gsel)
    el = jnp.where(emask, logits, ninf)
    v1 = jnp.max(el, axis=-1, keepdims=True)
    i1 = jnp.min(jnp.where(el == v1, lanef, big), axis=-1, keepdims=True)
    el2 = jnp.where(lanef == i1, ninf, el)
    v2 = jnp.max(el2, axis=-1, keepdims=True)
    i2 = jnp.min(jnp.where(el2 == v2, lanef, big), axis=-1, keepdims=True)
    e2 = jnp.exp(v2 - v1)
    w1 = (1.0 / (1.0 + e2)) * gw
    w2 = (e2 / (1.0 + e2)) * gw
    @pl.when(pl.program_id(0) == 0)
    def _():
        cnt_ref[...] = jnp.zeros_like(cnt_ref)

    pick1 = lanef == i1
    pick2 = lanef == i2
    picks = jnp.where(pick1 | pick2, 1.0, 0.0)
    before = jnp.dot(tri_ref[...], picks.astype(BF16), preferred_element_type=F32) + cnt_ref[...]
    r1 = jnp.sum(jnp.where(pick1, before, 0.0), axis=-1, keepdims=True)
    r2 = jnp.sum(jnp.where(pick2, before, 0.0), axis=-1, keepdims=True)
    cnt_ref[...] += jnp.sum(picks, axis=0, keepdims=True)
    route_ref[...] = (jnp.where(lane == 0, i1, 0.0) + jnp.where(lane == 1, i2, 0.0)
                      + jnp.where(lane == 2, w1, 0.0) + jnp.where(lane == 3, w2, 0.0)
                      + jnp.where(lane == 4, r1, 0.0) + jnp.where(lane == 5, r2, 0.0))


def _post(x2, att, lru, w, tm):
    t_tok = x2.shape[0]
    full = lambda a: pl.BlockSpec(a.shape, lambda i: (0,) * a.ndim)
    row = lambda width: pl.BlockSpec((tm, width), lambda i: (i, 0))
    tri = jnp.tril(jnp.ones((tm, tm), BF16), -1)
    return pl.pallas_call(
        _post_kernel,
        grid=(t_tok // tm,),
        in_specs=[row(D_MODEL), row(ATT_WIDTH), row(LRU_WIDTH), full(w["att_out_norm"]),
                  full(w["w_out"]), full(w["norm2"]), full(w["w_route"]), full(w["b_route"]),
                  full(tri)],
        out_specs=[row(D_MODEL), pl.BlockSpec((tm * SLAB, LANES), lambda i: (i, 0)), row(LANES),
                   pl.BlockSpec((1, LANES), lambda i: (0, 0))],
        out_shape=[jax.ShapeDtypeStruct((t_tok, D_MODEL), F32),
                   jax.ShapeDtypeStruct((t_tok * SLAB, LANES), F32),
                   jax.ShapeDtypeStruct((t_tok, LANES), F32),
                   jax.ShapeDtypeStruct((1, LANES), F32)],
        compiler_params=_params(("arbitrary",)),
        name="post",
    )(x2, att, lru, w["att_out_norm"], w["w_out"], w["norm2"], w["w_route"], w["b_route"], tri)


def _route_plan(route, counts, tr):
    t_tok = route.shape[0]
    ids = route[:, 0:2].astype(jnp.int32).T
    rank = route[:, 4:6].astype(jnp.int32).T
    counts = counts[0, :N_EXPERTS].astype(jnp.int32)
    padded = ((counts + tr - 1) // tr) * tr
    ends = jnp.cumsum(padded)
    starts = ends - padded
    experts = jnp.arange(N_EXPERTS, dtype=jnp.int32)
    group_start = jnp.sum(jnp.where(ids[..., None] == experts, starts, 0), axis=-1)
    pos = (group_start + rank).astype(jnp.int32)
    zero_start = jnp.where(padded > 0, ends - tr, -1).astype(jnp.int32)
    n_tiles = (2 * t_tok) // tr + N_EXPERTS
    tile_first = jnp.arange(n_tiles, dtype=jnp.int32) * tr
    tile_expert = jnp.minimum(jnp.sum((ends[None, :] <= tile_first[:, None]).astype(jnp.int32), axis=-1),
                              N_EXPERTS - 1).astype(jnp.int32)
    n_used = (ends[-1] // tr).astype(jnp.int32).reshape(1)
    return pos, zero_start, tile_expert, n_used


def _tile_major(pos, tm):
    return pos.reshape(2, -1, tm).transpose(1, 0, 2).reshape(-1)


def _row_copy(src, src_row, dst, dst_row, sem):
    s0 = pl.multiple_of(src_row * SLAB, SLAB)
    d0 = pl.multiple_of(dst_row * SLAB, SLAB)
    return pltpu.make_async_copy(src.at[pl.ds(s0, SLAB)], dst.at[pl.ds(d0, SLAB)], sem)


def _dispatch_kernel(zs_ref, pos_ref, t_ref, xs_hbm, zbuf, sem, zsem, *, tm, tr):
    i = pl.program_id(0)

    def zero_copy(e):
        z0 = pl.multiple_of(zs_ref[e] * SLAB, tr * SLAB)
        return pltpu.make_async_copy(zbuf, xs_hbm.at[pl.ds(z0, tr * SLAB)], zsem)

    @pl.when(i == 0)
    def _():
        zbuf[...] = jnp.zeros_like(zbuf)
        for e in range(N_EXPERTS):
            @pl.when(zs_ref[e] >= 0)
            def _():
                zero_copy(e).start()
        for e in range(N_EXPERTS):
            @pl.when(zs_ref[e] >= 0)
            def _():
                zero_copy(e).wait()

    def issue(j, carry):
        _row_copy(t_ref, j, xs_hbm, pos_ref[j], sem).start(priority=0)
        _row_copy(t_ref, j, xs_hbm, pos_ref[tm + j], sem).start(priority=1)
        return carry

    lax.fori_loop(0, tm, issue, 0, unroll=8)

    def drain(j, carry):
        _row_copy(t_ref, 0, xs_hbm, 0, sem).wait()
        _row_copy(t_ref, 0, xs_hbm, 0, sem).wait()
        return carry

    lax.fori_loop(0, tm, drain, 0, unroll=8)


def _dispatch(t, pos_tiles, zero_start, n_rows, tm, tr):
    t_tok = t.shape[0] // SLAB
    return pl.pallas_call(
        functools.partial(_dispatch_kernel, tm=tm, tr=tr),
        grid_spec=pltpu.PrefetchScalarGridSpec(
            num_scalar_prefetch=1,
            grid=(t_tok // tm,),
            in_specs=[pl.BlockSpec((2 * tm,), lambda i, zs: (i,), memory_space=pltpu.SMEM),
                      pl.BlockSpec((tm * SLAB, LANES), lambda i, zs: (i, 0))],
            out_specs=pl.BlockSpec(memory_space=pl.ANY),
            scratch_shapes=[pltpu.VMEM((tr * SLAB, LANES), F32), pltpu.SemaphoreType.DMA(()),
                            pltpu.SemaphoreType.DMA(())]),
        out_shape=jax.ShapeDtypeStruct((n_rows * SLAB, LANES), F32),
        compiler_params=_params(("arbitrary",)),
        name="dispatch",
    )(zero_start, pos_tiles, t)


def _experts_kernel(te_ref, nu_ref, xs_ref, wgu_ref, wd_ref, ys_ref, *, tr):
    @pl.when(pl.program_id(0) < nu_ref[0])
    def _():
        x = _load_slabs(xs_ref, 0, tr).astype(BF16)
        gu = jnp.dot(x, wgu_ref[0], preferred_element_type=F32)
        a = gu[:, :D_EXPERT]
        h = (a * _sigmoid(a)) * gu[:, D_EXPERT:]
        _store_slabs(ys_ref, 0, jnp.dot(h.astype(BF16), wd_ref[0], preferred_element_type=F32))


def _experts(xs, tile_expert, n_used, w, tr):
    n_rows = xs.shape[0] // SLAB
    row = pl.BlockSpec((tr * SLAB, LANES), lambda i, te, nu: (jnp.minimum(i, nu[0] - 1), 0))
    return pl.pallas_call(
        functools.partial(_experts_kernel, tr=tr),
        grid_spec=pltpu.PrefetchScalarGridSpec(
            num_scalar_prefetch=2,
            grid=(n_rows // tr,),
            in_specs=[row,
                      pl.BlockSpec((1, D_MODEL, 2 * D_EXPERT), lambda i, te, nu: (te[i], 0, 0)),
                      pl.BlockSpec((1, D_EXPERT, D_MODEL), lambda i, te, nu: (te[i], 0, 0))],
            out_specs=row),
        out_shape=jax.ShapeDtypeStruct((n_rows * SLAB, LANES), F32),
        compiler_params=_params(("arbitrary",)),
        name="experts",
    )(tile_expert, n_used, xs, w["w_gu"], w["w_down"])


def _combine_kernel(pos_ref, nxt_ref, x1_ref, route_ref, fn_ref, ys_hbm, o_ref, ybuf, sem, *, tm):
    i = pl.program_id(0)
    n = pl.num_programs(0)
    slot = i % 2

    def issue(p_ref, s):
        def body(j, carry):
            _row_copy(ys_hbm, p_ref[j], ybuf.at[s], j, sem.at[s]).start(priority=0)
            _row_copy(ys_hbm, p_ref[tm + j], ybuf.at[s], tm + j, sem.at[s]).start(priority=1)
            return carry
        lax.fori_loop(0, tm, body, 0, unroll=8)

    @pl.when(i == 0)
    def _():
        issue(pos_ref, 0)

    @pl.when(i + 1 < n)
    def _():
        issue(nxt_ref, 1 - slot)

    def drain(j, carry):
        _row_copy(ys_hbm, 0, ybuf.at[slot], 0, sem.at[slot]).wait()
        _row_copy(ys_hbm, 0, ybuf.at[slot], 0, sem.at[slot]).wait()
        return carry

    lax.fori_loop(0, tm, drain, 0, unroll=8)

    r = route_ref[...]
    yb = ybuf.at[slot]
    moe = r[:, 2:3] * _load_slabs(yb, 0, tm) + r[:, 3:4] * _load_slabs(yb, tm * SLAB, tm)
    o_ref[...] = _rms(x1_ref[...] + moe) * fn_ref[...]


def _combine(ys, pos_tiles, x1, route, w, tm):
    t_tok = x1.shape[0]
    nt = t_tok // tm
    row = lambda width: pl.BlockSpec((tm, width), lambda i: (i, 0))
    return pl.pallas_call(
        functools.partial(_combine_kernel, tm=tm),
        grid=(nt,),
        in_specs=[pl.BlockSpec((2 * tm,), lambda i: (i,), memory_space=pltpu.SMEM),
                  pl.BlockSpec((2 * tm,), lambda i: (jnp.minimum(i + 1, nt - 1),),
                               memory_space=pltpu.SMEM),
                  row(D_MODEL), row(LANES), pl.BlockSpec((1, D_MODEL), lambda i: (0, 0)),
                  pl.BlockSpec(memory_space=pl.ANY)],
        out_specs=row(D_MODEL),
        out_shape=jax.ShapeDtypeStruct((t_tok, D_MODEL), F32),
        scratch_shapes=[pltpu.VMEM((2, 2 * tm * SLAB, LANES), F32),
                        pltpu.SemaphoreType.DMA((2,))],
        compiler_params=_params(("arbitrary",)),
        name="combine",
    )(pos_tiles, pos_tiles, x1, route, w["final_norm"], ys)


def _moe(t, route, counts, x1, w, tm_dispatch, tm_combine, tr):
    t_tok = x1.shape[0]
    n_rows = 2 * t_tok + N_EXPERTS * tr
    pos, zero_start, tile_expert, n_used = _route_plan(route, counts, tr)
    xs = _dispatch(t, _tile_major(pos, tm_dispatch), zero_start, n_rows, tm_dispatch, tr)
    ys = _experts(xs, tile_expert, n_used, w, tr)
    return _combine(ys, _tile_major(pos, tm_combine), x1, route, w, tm_combine)


def _rope_tables(seq):
    pos = jnp.arange(seq, dtype=F32)
    inv_freq = 1.0 / (ROPE_BASE ** (jnp.arange(0, QK_ROPE, 2, dtype=F32) / QK_ROPE))
    freqs = pos[:, None] * inv_freq[None, :]
    cos2 = jnp.concatenate([jnp.cos(freqs), jnp.cos(freqs)], axis=-1)
    sin2 = jnp.concatenate([jnp.sin(freqs), jnp.sin(freqs)], axis=-1)
    scale = (QK_NOPE + QK_ROPE) ** -0.5 * math.log2(math.e)
    pad =jnp.zeros((seq, HEAD_PAD - QK_NOPE - QK_ROPE), F32)
    ones = jnp.ones((seq, QK_NOPE), F32)
    zeros = jnp.zeros((seq, QK_NOPE), F32)
    cq = jnp.concatenate([ones, cos2, pad], axis=-1) * scale
    sq = jnp.concatenate([zeros, sin2, pad], axis=-1) * scale
    ck = jnp.concatenate([zeros, cos2, pad], axis=-1)
    sk = jnp.concatenate([zeros, sin2, pad], axis=-1)
    return cq, sq, ck, sk


def _rot_half_cols(wp):
    hr = QK_ROPE // 2
    return jnp.concatenate([-wp[..., hr:], wp[..., :hr]], axis=-1)


def _layout_weights(norm1, w_in, q_norm, w_q_up, kv_norm, w_kv_up, conv_w, conv_b, w_rg, b_rg,
                    w_ig, b_ig, lru_lambda, att_out_norm, lru_out_norm, w_out, norm2, w_group,
                    b_group, w_router, b_router, w_gate, w_up, w_down, final_norm):
    row = lambda a: a.reshape(1, -1)
    c1 = Q_LORA
    c2 = c1 + KV_LORA
    c3 = c2 + QK_ROPE
    c4 = c3 + LRU_WIDTH
    w_kr = w_in[:, c2:c3]
    zl = jnp.zeros((D_MODEL, QK_NOPE), F32)
    zr = jnp.zeros((D_MODEL, HEAD_PAD - QK_NOPE - QK_ROPE), F32)
    w_in2 = jnp.concatenate([w_in[:, :c2], w_in[:, c3:c4], w_in[:, c4:],
                             zl, w_kr, zr, zl, _rot_half_cols(w_kr), zr], axis=-1)

    wq = w_q_up.reshape(Q_LORA, N_HEADS, QK_NOPE + QK_ROPE)
    zq = jnp.zeros((Q_LORA, N_HEADS, HEAD_PAD - QK_NOPE - QK_ROPE), F32)
    wq_plain = jnp.concatenate([wq, zq], axis=-1).reshape(Q_LORA, N_HEADS * HEAD_PAD)
    wq_rot = jnp.concatenate([jnp.zeros((Q_LORA, N_HEADS, QK_NOPE), F32),
                              _rot_half_cols(wq[..., QK_NOPE:]), zq], axis=-1)
    w_q = jnp.concatenate([wq_plain, wq_rot.reshape(Q_LORA, N_HEADS * HEAD_PAD)], axis=-1)

    wkv = w_kv_up.reshape(KV_LORA, N_HEADS, QK_NOPE + V_HEAD)
    w_k = jnp.concatenate([wkv[..., :QK_NOPE],
                           jnp.zeros((KV_LORA, N_HEADS, HEAD_PAD - QK_NOPE), F32)],
                          axis=-1).reshape(KV_LORA, N_HEADS * HEAD_PAD)
    w_v = wkv[..., QK_NOPE:].reshape(KV_LORA, N_HEADS * V_HEAD)

    def block_diag(wb):
        eye = jnp.eye(LRU_BLOCKS, dtype=F32)
        return jnp.einsum('ncd,nm->ncmd', wb, eye).reshape(LRU_WIDTH, LRU_WIDTH)

    w_gates = jnp.concatenate([block_diag(w_rg[0]), block_diag(w_ig[0]),
                               block_diag(w_rg[1]), block_diag(w_ig[1])], axis=-1)
    b_gates = jnp.concatenate([b_rg[0], b_ig[0], b_rg[1], b_ig[1]]).reshape(1, -1)

    w_r = jnp.concatenate([w_router, w_group,
                           jnp.zeros((D_MODEL, LANES - N_EXPERTS - N_GROUPS), F32)], axis=-1)
    w_r_hi = w_r.astype(BF16)
    w_r_lo = (w_r - w_r_hi.astype(F32)).astype(BF16)
    b_r = jnp.concatenate([b_router, b_group,
                           jnp.zeros((LANES - N_EXPERTS - N_GROUPS,), F32)]).reshape(1, -1)

    return {
        "norm1": row(norm1), "w_in": w_in2.astype(BF16), "q_norm": row(q_norm),
        "w_q": w_q.astype(BF16), "kv_norm": row(kv_norm), "w_k": w_k.astype(BF16),
        "w_v": w_v.astype(BF16),
        "conv_w": conv_w, "conv_b": row(conv_b), "w_gates": w_gates.astype(BF16),
        "b_gates": b_gates, "lru_lambda": lru_lambda, "lru_out_norm": row(lru_out_norm),
        "att_out_norm": row(att_out_norm), "w_out": w_out.astype(BF16), "norm2": row(norm2),
        "w_route": jnp.concatenate([w_r_hi, w_r_lo], axis=-1), "b_route": b_r,
        "w_gu": jnp.concatenate([w_gate, w_up], axis=-1).astype(BF16),
        "w_down": w_down.astype(BF16), "final_norm": row(final_norm),
    }


def _tile(n, pref):
    return pref if n % pref == 0 else n


def _trunk(x, w):
    batch, seq, _ = x.shape
    t_tok = batch * seq
    x2 = x.reshape(t_tok, D_MODEL)
    wt = dict(w)
    wt["cq"], wt["sq"], wt["ck"], wt["sk"] = _rope_tables(seq)
    q, k, v, xbr, gbr = _proj_in(x2, seq, wt, _tile(seq, 512))
    att = _attention(q, k, v, batch, seq, _tile(seq, 512), 4)
    lru = _lru(xbr, gbr, wt, batch, seq, min(512, seq // 2))
    x1, t, route, counts = _post(x2, att, lru, wt, _tile(t_tok, 512))
    y = _moe(t, route, counts, x1, wt, _tile(t_tok, 1024), _tile(t_tok, 512), _tile(t_tok, 512))
    return y.reshape(batch, seq, D_MODEL)


def kernel(x_prompt, x_sample, norm1, w_in, q_norm, w_q_up, kv_norm, w_kv_up, conv_w, conv_b, w_rg, b_rg, w_ig, b_ig, lru_lambda, att_out_norm, lru_out_norm, w_out, norm2, w_group, b_group, w_router, b_router, w_gate, w_up, w_down, final_norm):
    w = _layout_weights(norm1[0], w_in[0], q_norm[0], w_q_up[0], kv_norm[0], w_kv_up[0], conv_w[0],
                        conv_b[0], w_rg[0], b_rg[0], w_ig[0], b_ig[0], lru_lambda[0],
                        att_out_norm[0], lru_out_norm[0], w_out[0], norm2[0], w_group[0],
                        b_group[0], w_router[0], b_router[0], w_gate[0], w_up[0], w_down[0],
                        final_norm)
    return (_trunk(x_prompt, w), _trunk(x_sample, w))
```
